```python
import jax, jax.numpy as jnp
from jax import lax
import numpy as np

D_MODEL = 1024
BATCH = 8
SEQ = 4096
DEPTH = 2

CHUNK = 64
N_MIXERS = 2
CONV_WIDTH = 31
HGRN_EXPAND = 128
HGRN_HEADS = D_MODEL // HGRN_EXPAND
HEAD_K = HGRN_EXPAND
HEAD_V = D_MODEL // HGRN_HEADS
FFN_DIM = 2816
N_EXPERTS = 8
TOP_K = 2
EXPERT_DIM = 3584
N_EVEN = (DEPTH + 1) // 2
N_ODD = DEPTH // 2
EPS = 1e-6

kernel_name = "hybrid_conformer_hgrn2_moe_trunk"


def rms_norm(x, g):
    xf = x.astype(jnp.float32)
    y = xf * lax.rsqrt(jnp.mean(xf * xf, axis=-1, keepdims=True) + EPS)
    return (y * g.astype(jnp.float32)).astype(x.dtype)


def layer_norm(x, g, b):
    xf = x.astype(jnp.float32)
    mu = jnp.mean(xf, axis=-1, keepdims=True)
    xc = xf - mu
    y = xc * lax.rsqrt(jnp.mean(xc * xc, axis=-1, keepdims=True) + EPS)
    return (y * g.astype(jnp.float32) + b.astype(jnp.float32)).astype(x.dtype)


def causal_depthwise_conv(u, w, b):
    width = w.shape[0]
    y = lax.conv_general_dilated(
        u, w[:, None, :].astype(u.dtype), window_strides=(1,),
        padding=[(width - 1, 0)], dimension_numbers=('NWC', 'WIO', 'NWC'),
        feature_group_count=u.shape[-1])
    return y + b


def conformer_conv_module(h, w_in, b_in, dw_w, dw_b, ln_g, ln_b, w_out, b_out):
    u = h @ w_in + b_in
    a, gate = jnp.split(u, 2, axis=-1)
    u = a * jax.nn.sigmoid(gate)
    u = causal_depthwise_conv(u, dw_w, dw_b)
    u = jax.nn.silu(layer_norm(u, ln_g, ln_b))
    return u @ w_out + b_out


def gated_linear_recurrence_chunkwise(q, k, v, g):
    b_, l_, h_, kd = q.shape
    vd = v.shape[-1]
    n_chunks = l_ // CHUNK

    def to_chunks(t):
        return t.reshape(b_, n_chunks, CHUNK, h_, t.shape[-1]).transpose(1, 0, 3, 2, 4)

    qc, kc, vc, gc = to_chunks(q), to_chunks(k), to_chunks(v), to_chunks(g)
    causal = jnp.tril(jnp.ones((CHUNK, CHUNK), dtype=bool))[:, :, None]

    def step(state, inp):
        qb, kb, vb, gb = inp
        big_g = jnp.cumsum(gb, axis=2)
        o_inter = jnp.einsum('bhtk,bhkv->bhtv', qb * jnp.exp(big_g), state)
        rel = big_g[:, :, :, None, :] - big_g[:, :, None, :, :]
        decay = jnp.exp(jnp.where(causal, rel, -jnp.inf))
        scores = jnp.einsum('bhtk,bhsk,bhtsk->bhts', qb, kb, decay)
        o_intra = jnp.einsum('bhts,bhsv->bhtv', scores, vb)
        g_last = big_g[:, :, -1:, :]
        k_dec = kb * jnp.exp(g_last - big_g)
        state = (jnp.exp(g_last[:, :, 0, :])[..., None] * state
                 + jnp.einsum('bhsk,bhsv->bhkv', k_dec, vb))
        return state, o_inter + o_intra

    s0 = jnp.zeros((b_, h_, kd, vd), jnp.float32)
    _, o = lax.scan(step, s0, (qc, kc, vc, gc))
    return o.transpose(1, 0, 3, 2, 4).reshape(b_, l_, h_, vd)


def hgrn2_mixer(h, w_in, lower_bound, norm_g, w_out):
    b_, l_, _ = h.shape
    proj = h @ w_in
    q, f, i, out_gate = jnp.split(proj, 4, axis=-1)
    q = jax.nn.silu(q).astype(jnp.float32).reshape(b_, l_, HGRN_HEADS, HEAD_K)
    lb = lower_bound.astype(jnp.float32)
    forget = lb + (1.0 - lb) * jax.nn.sigmoid(f.astype(jnp.float32))
    k = (1.0 - forget).reshape(b_, l_, HGRN_HEADS, HEAD_K)
    g = jnp.log(forget).reshape(b_, l_, HGRN_HEADS, HEAD_K)
    v = i.astype(jnp.float32).reshape(b_, l_, HGRN_HEADS, HEAD_V)
    o = gated_linear_recurrence_chunkwise(q, k, v, g)
    o = rms_norm(o, norm_g).reshape(b_, l_, D_MODEL).astype(h.dtype)
    o = o * jax.nn.silu(out_gate)
    return o @ w_out


def swiglu(h, w_gu, w_down):
    gate, up = jnp.split(h @ w_gu, 2, axis=-1)
    return (jax.nn.silu(gate) * up) @ w_down


def moe_swiglu(h, w_router, w_gu, w_down):
    logits = jnp.einsum('bld,de->ble', h, w_router).astype(jnp.float32)
    top_val, top_idx = lax.top_k(logits, TOP_K)
    top_w = jax.nn.softmax(top_val, axis=-1)
    gates = jnp.sum(jax.nn.one_hot(top_idx, N_EXPERTS, dtype=jnp.float32)
                    * top_w[..., None], axis=-2)
    out = jnp.zeros_like(h)
    for e in range(N_EXPERTS):
        out = out + gates[..., e:e + 1].astype(h.dtype) * swiglu(h, w_gu[e], w_down[e])
    return out


def setup_inputs(seed: int = 0) -> dict:
    key = jax.random.key(seed)
    ks = jax.random.split(key, 24)
    d, f32 = D_MODEL, jnp.float32

    def nrm(k, shape, fan_in, scale=1.0):
        return jax.random.normal(k, shape, f32) * (scale * fan_in ** -0.5)

    def gain(k, shape):
        return 1.0 + 0.05 * jax.random.normal(k, shape, f32)

    def bias(k, shape):
        return 0.02 * jax.random.normal(k, shape, f32)

    return {
        "x": jax.random.normal(ks[0], (BATCH, SEQ, d), f32),
        "norm_g": gain(ks[1], (DEPTH, 4, d)),
        "conv_w_in": nrm(ks[2], (N_EVEN, d, 2 * d), d),
        "conv_b_in": bias(ks[3], (N_EVEN, 2 * d)),
        "conv_dw_w": nrm(ks[4], (N_EVEN, CONV_WIDTH, d), CONV_WIDTH),
        "conv_dw_b": bias(ks[5], (N_EVEN, d)),
        "conv_ln_g": gain(ks[6], (N_EVEN, d)),
        "conv_ln_b": bias(ks[7], (N_EVEN, d)),
        "conv_w_out": nrm(ks[8], (N_EVEN, d, d), d),
        "conv_b_out": bias(ks[9], (N_EVEN, d)),
        "hgrn_w_in": nrm(ks[10], (N_ODD, d, 4 * d), d),
        "hgrn_lower_bounds": 0.5 * jax.random.normal(ks[11], (DEPTH, d), f32),
        "hgrn_norm_g": gain(ks[12], (N_ODD, HEAD_V)),
        "hgrn_w_out": nrm(ks[13], (N_ODD, d, d), d),
        "ffn_w_gu": nrm(ks[14], (N_EVEN, d, 2 * FFN_DIM), d),
        "ffn_w_down": nrm(ks[15], (N_EVEN, FFN_DIM, d), FFN_DIM),
        "moe_router": nrm(ks[16], (N_ODD, d, N_EXPERTS), d),
        "moe_w_gu": nrm(ks[17], (N_ODD, N_EXPERTS, d, 2 * EXPERT_DIM), d),
        "moe_w_down": nrm(ks[18], (N_ODD, N_EXPERTS, EXPERT_DIM, d), EXPERT_DIM),
    }


def reference(x, norm_g, conv_w_in, conv_b_in, conv_dw_w, conv_dw_b, conv_ln_g, conv_ln_b,
              conv_w_out, conv_b_out, hgrn_w_in, hgrn_lower_bounds, hgrn_norm_g, hgrn_w_out,
              ffn_w_gu, ffn_w_down, moe_router, moe_w_gu, moe_w_down):
    lb_all = jnp.cumsum(jax.nn.softmax(hgrn_lower_bounds.astype(jnp.float32), axis=0), axis=0)
    lb_all = lb_all - lb_all[0:1]
    h = x
    for i in range(DEPTH):
        j = i // 2
        u = rms_norm(h, norm_g[i, 0])
        if i % N_MIXERS == 0:
            u = conformer_conv_module(u, conv_w_in[j], conv_b_in[j], conv_dw_w[j], conv_dw_b[j],
                                      conv_ln_g[j], conv_ln_b[j], conv_w_out[j], conv_b_out[j])
        else:
            u = hgrn2_mixer(u, hgrn_w_in[j], lb_all[i], hgrn_norm_g[j], hgrn_w_out[j])
        h = h + rms_norm(u, norm_g[i, 1])
        u = rms_norm(h, norm_g[i, 2])
        if i % 2 == 0:
            u = swiglu(u, ffn_w_gu[j], ffn_w_down[j])
        else:
            u = moe_swiglu(u, moe_router[j], moe_w_gu[j], moe_w_down[j])
        h = h + rms_norm(u, norm_g[i, 3])
    return h
```

```python
import functools

import jax
import jax.numpy as jnp
from jax import lax
from jax.experimental import pallas as pl
from jax.experimental.pallas import tpu as pltpu

EPS = 1e-6
CHUNK = 64
SUB = 16
HEAD = 128
CONV_WIDTH = 31
HALO = 32
CONV_ROWS = 32
LANES = 128
TOP_K = 2
EXP_CLAMP = 80.0
VMEM_LIMIT = 56 * 1024 * 1024

f32 = jnp.float32
bf16 = jnp.bfloat16


def _dot(a, b):
    return jnp.dot(a, b, preferred_element_type=f32)


def _dot_nt(a, b):
    return lax.dot_general(a, b, (((1,), (1,)), ((), ())), preferred_element_type=f32)


def _dot_tn(a, b):
    return lax.dot_general(a, b, (((0,), (0,)), ((), ())), preferred_element_type=f32)


def _rms(x, g):
    return x * lax.rsqrt(jnp.mean(x * x, axis=-1, keepdims=True) + EPS) * g


def _silu(x):
    return x * jax.nn.sigmoid(x)


def _const_spec(shape):
    nd = len(shape)
    return pl.BlockSpec(shape, lambda *_: (0,) * nd, pipeline_mode=pl.Buffered(1))


def _conv_mixer_kernel(x_ref, gpre_ref, gpost_ref, win_ref, bin_ref, dww_ref, dwb_ref,
                       lng_ref, lnb_ref, wout_ref, bout_ref, o_ref, cbuf_ref, z_ref,
                       *, tm, d):
    i = pl.program_id(1)

    @pl.when(i == 0)
    def _():
        cbuf_ref[0:HALO, :] = jnp.zeros((HALO, d), f32)

    @pl.when(i > 0)
    def _():
        cbuf_ref[0:HALO, :] = cbuf_ref[tm:tm + HALO, :]

    x = x_ref[...]
    u = _rms(x, gpre_ref[...]).astype(bf16)
    a = _dot(u, win_ref[:, :d]) + bin_ref[:, :d]
    gate = _dot(u, win_ref[:, d:]) + bin_ref[:, d:]
    cbuf_ref[HALO:HALO + tm, :] = a * jax.nn.sigmoid(gate)

    off = HALO - (CONV_WIDTH - 1)
    for rb in range(tm // CONV_ROWS):
        base = rb * CONV_ROWS
        acc = jnp.broadcast_to(dwb_ref[...], (CONV_ROWS, d))
        for j in range(CONV_WIDTH):
            acc = acc + dww_ref[j:j + 1, :] * cbuf_ref[base + off + j:base + off + j + CONV_ROWS, :]
        mu = jnp.mean(acc, axis=-1, keepdims=True)
        xc = acc - mu
        y = xc * lax.rsqrt(jnp.mean(xc * xc, axis=-1, keepdims=True) + EPS)
        y = y * lng_ref[...] + lnb_ref[...]
        z_ref[base:base + CONV_ROWS, :] = _silu(y).astype(bf16)

    out = _dot(z_ref[...], wout_ref[...]) + bout_ref[...]
    o_ref[...] = x + _rms(out, gpost_ref[...])


def _conv_mixer(x, gpre, gpost, w_in, b_in, dw_w, dw_b, ln_g, ln_b, w_out, b_out, *, tm):
    b, l, d = x.shape
    row = lambda v: v.reshape(1, -1)
    kern = functools.partial(_conv_mixer_kernel, tm=tm, d=d)
    return pl.pallas_call(
        kern,
        grid=(b, l // tm),
        in_specs=[
            pl.BlockSpec((None, tm, d), lambda bi, i: (bi, i, 0)),
            _const_spec((1, d)), _const_spec((1, d)),
            _const_spec((d, 2 * d)), _const_spec((1, 2 * d)),
            _const_spec((CONV_WIDTH, d)), _const_spec((1, d)),
            _const_spec((1, d)), _const_spec((1, d)),
            _const_spec((d, d)), _const_spec((1, d)),
        ],
        out_specs=pl.BlockSpec((None, tm, d), lambda bi, i: (bi, i, 0)),
        out_shape=jax.ShapeDtypeStruct((b, l, d), f32),
        scratch_shapes=[pltpu.VMEM((tm + HALO, d), f32), pltpu.VMEM((tm, d), bf16)],
        compiler_params=pltpu.CompilerParams(
            dimension_semantics=("arbitrary", "arbitrary"), vmem_limit_bytes=VMEM_LIMIT),
        name="conv_mixer",
    )(x, row(gpre), row(gpost), w_in, row(b_in), dw_w, row(dw_b), row(ln_g), row(ln_b),
      w_out, row(b_out))


def _ffn_kernel(h_ref, gpre_ref, gpost_ref, wgu_ref, wd_ref, o_ref, a_ref, *, f, fc):
    x = h_ref[...]
    u = _rms(x, gpre_ref[...]).astype(bf16)
    for c in range(f // fc):
        gate = _dot(u, wgu_ref[:, c * fc:(c + 1) * fc])
        up = _dot(u, wgu_ref[:, f + c * fc:f + (c + 1) * fc])
        a_ref[:, c * fc:(c + 1) * fc] = (_silu(gate) * up).astype(bf16)
    out = _dot(a_ref[...], wd_ref[...])
    o_ref[...] = x + _rms(out, gpost_ref[...])


def _ffn(h, gpre, gpost, w_gu, w_down, *, tm, fc):
    t, d = h.shape
    f = w_down.shape[0]
    kern = functools.partial(_ffn_kernel, f=f, fc=fc)
    return pl.pallas_call(
        kern,
        grid=(t // tm,),
        in_specs=[
            pl.BlockSpec((tm, d), lambda i: (i, 0)),
            _const_spec((1, d)), _const_spec((1, d)),
            _const_spec((d, 2 * f)), _const_spec((f, d)),
        ],
        out_specs=pl.BlockSpec((tm, d), lambda i: (i, 0)),
        out_shape=jax.ShapeDtypeStruct((t, d), f32),
        scratch_shapes=[pltpu.VMEM((tm, f), bf16)],
        compiler_params=pltpu.CompilerParams(
            dimension_semantics=("arbitrary",), vmem_limit_bytes=VMEM_LIMIT),
        name="dense_ffn",
    )(h, gpre.reshape(1, d), gpost.reshape(1, d), w_gu, w_down)


def _hgrn_kernel(h_ref, gpre_ref, gpost_ref, win_ref, lbp_ref, ng_ref, wout_ref, o_ref,
                 q_ref, k_ref, v_ref, g_ref, og_ref, oh_ref, st_ref, *, tm, d, layer):
    i = pl.program_id(1)
    heads = d // HEAD
    nsub = CHUNK // SUB

    @pl.when(i == 0)
    def _():
        st_ref[...] = jnp.zeros(st_ref.shape, f32)

    x = h_ref[...]
    u = _rms(x, gpre_ref[...]).astype(bf16)

    lbp = lbp_ref[...]
    e = jnp.exp(lbp - jnp.max(lbp, axis=0, keepdims=True))
    sm = e / jnp.sum(e, axis=0, keepdims=True)
    lb = jnp.sum(sm[1:layer + 1, :], axis=0, keepdims=True)

    q_ref[...] = _silu(_dot(u, win_ref[:, 0:d]))
    forget = lb + (1.0 - lb) * jax.nn.sigmoid(_dot(u, win_ref[:, d:2 * d]))
    k_ref[...] = 1.0 - forget
    g_ref[...] = jnp.log(forget)
    v_ref[...] = _dot(u, win_ref[:, 2 * d:3 * d])
    og_ref[...] = _silu(_dot(u, win_ref[:, 3 * d:4 * d]))

    rr = lax.broadcasted_iota(jnp.int32, (CHUNK, CHUNK), 0)
    cc = lax.broadcasted_iota(jnp.int32, (CHUNK, CHUNK), 1)
    tri = (rr >= cc).astype(bf16)
    diag_mask = (rr >= cc) & ((rr // SUB) == (cc // SUB))
    rblk = lax.broadcasted_iota(jnp.int32, (CHUNK, HEAD), 0) // SUB
    ng = ng_ref[...]

    def chunk_body(c, carry):
        r0 = pl.multiple_of(c * CHUNK, CHUNK)
        g = g_ref[pl.ds(r0, CHUNK), :]
        g_hi = g.astype(bf16)
        g_lo = (g - g_hi.astype(f32)).astype(bf16)
        big_g = _dot(tri, g_hi) + _dot(tri, g_lo)
        for hd in range(heads):
            sl = slice(hd * HEAD, (hd + 1) * HEAD)
            gh = big_g[:, sl]
            q = q_ref[pl.ds(r0, CHUNK), sl]
            k = k_ref[pl.ds(r0, CHUNK), sl]
            v = v_ref[pl.ds(r0, CHUNK), sl].astype(bf16)
            st = st_ref[hd]
            g_last = gh[CHUNK - 1:CHUNK, :]

            o = _dot_nt((q * jnp.exp(gh)).astype(bf16), st.astype(bf16))

            ends = [gh[SUB * b + SUB - 1:SUB * b + SUB, :] for b in range(nsub)]
            ref_end = jnp.concatenate([jnp.broadcast_to(r, (SUB, HEAD)) for r in ends], axis=0)
            mids = [0.5 * (gh[SUB * b:SUB * b + 1, :] + ends[b]) for b in range(nsub)]
            ref_mid = jnp.concatenate([jnp.broadcast_to(r, (SUB, HEAD)) for r in mids], axis=0)
            k_end = k * jnp.exp(ref_end - gh)
            q_parts, k_parts = [], []
            for b in range(nsub - 1):
                qb = q * jnp.exp(jnp.minimum(gh - ends[b], 0.0))
                q_parts.append(jnp.where(rblk > b, qb, 0.0))
                k_parts.append(jnp.where(rblk == b, k_end, 0.0))
            s_off = _dot_nt(jnp.concatenate(q_parts, axis=1).astype(bf16),
                            jnp.concatenate(k_parts, axis=1).astype(bf16))
            q_mid = q * jnp.exp(jnp.clip(gh - ref_mid, -EXP_CLAMP, EXP_CLAMP))
            k_mid = k * jnp.exp(jnp.clip(ref_mid - gh, -EXP_CLAMP, EXP_CLAMP))
            s_diag = _dot_nt(q_mid.astype(bf16), k_mid.astype(bf16))
            scores = s_off + jnp.where(diag_mask, s_diag, 0.0)
            o = o + _dot(scores.astype(bf16), v)

            k_dec = (k * jnp.exp(g_last - gh)).astype(bf16)
            st_ref[hd] = jnp.exp(g_last) * st + _dot_tn(v, k_dec)

            o = o * lax.rsqrt(jnp.mean(o * o, axis=-1, keepdims=True) + EPS) * ng
            oh_ref[pl.ds(r0, CHUNK), sl] = (o * og_ref[pl.ds(r0, CHUNK), sl]).astype(bf16)
        return carry

    lax.fori_loop(0, tm // CHUNK, chunk_body, 0)
    out = _dot(oh_ref[...], wout_ref[...])
    o_ref[...] = x + _rms(out, gpost_ref[...])


def _hgrn_mixer(h, gpre, gpost, w_in, lower_bounds, norm_g, w_out, *, tm, layer):
    b, l, d = h.shape
    depth = lower_bounds.shape[0]
    kern = functools.partial(_hgrn_kernel, tm=tm, d=d, layer=layer)
    return pl.pallas_call(
        kern,
        grid=(b, l // tm),
        in_specs=[
            pl.BlockSpec((None, tm, d), lambda bi, i: (bi, i, 0)),
            _const_spec((1, d)), _const_spec((1, d)),
            _const_spec((d, 4 * d)), _const_spec((depth, d)), _const_spec((1, HEAD)),
            _const_spec((d, d)),
        ],
        out_specs=pl.BlockSpec((None, tm, d), lambda bi, i: (bi, i, 0)),
        out_shape=jax.ShapeDtypeStruct((b, l, d), f32),
        scratch_shapes=[pltpu.VMEM((tm, d), f32) for _ in range(5)]
        + [pltpu.VMEM((tm, d), bf16), pltpu.VMEM((d // HEAD, HEAD, HEAD), f32)],
        compiler_params=pltpu.CompilerParams(
            dimension_semantics=("arbitrary", "arbitrary"), vmem_limit_bytes=VMEM_LIMIT),
        name="hgrn_mixer",
    )(h, gpre.reshape(1, d), gpost.reshape(1, d), w_in, lower_bounds, norm_g.reshape(1, HEAD), w_out)


def _router_kernel(h_ref, gpre_ref, wr_ref, up_ref, pos_ref, gate_ref, cnt_ref, carry_ref,
                   *, tm, d, n_exp):
    i = pl.program_id(0)

    @pl.when(i == 0)
    def _():
        carry_ref[...] = jnp.zeros(carry_ref.shape, f32)

    u = _rms(h_ref[...], gpre_ref[...])
    bits = pltpu.bitcast(u.astype(bf16).astype(f32), jnp.uint32)
    half = d // 2
    up_ref[...] = (bits[:, :half] >> 16) | (bits[:, half:] & jnp.uint32(0xFFFF0000))

    logits = jnp.dot(u, wr_ref[...], preferred_element_type=f32, precision=lax.Precision.HIGHEST)
    lane = lax.broadcasted_iota(jnp.int32, (tm, LANES), 1)
    neg = jnp.float32(-jnp.inf)
    lg = jnp.where(lane < n_exp, logits, neg)
    v1 = jnp.max(lg, axis=-1, keepdims=True)
    i1 = jnp.min(jnp.where(lg == v1, lane, LANES), axis=-1, keepdims=True)
    m1 = lane == i1
    lg2 = jnp.where(m1, neg, lg)
    v2 = jnp.max(lg2, axis=-1, keepdims=True)
    i2 = jnp.min(jnp.where(lg2 == v2, lane, LANES), axis=-1, keepdims=True)
    m2 = lane == i2
    dd = jnp.exp(v2 - v1)
    w1 = 1.0 / (1.0 + dd)
    w2 = dd / (1.0 + dd)

    sel = (m1 | m2).astype(f32)
    rr = lax.broadcasted_iota(jnp.int32, (tm, tm), 0)
    cc = lax.broadcasted_iota(jnp.int32, (tm, tm), 1)
    cum = _dot((rr >= cc).astype(bf16), sel.astype(bf16))
    rank = cum - sel + carry_ref[...]
    r1 = jnp.sum(jnp.where(m1, rank, 0.0), axis=-1, keepdims=True).astype(jnp.int32)
    r2 = jnp.sum(jnp.where(m2, rank, 0.0), axis=-1, keepdims=True).astype(jnp.int32)
    pos_ref[...] = jnp.where(lane == 0, r1, jnp.where(lane == 1, r2, jnp.where(
        lane == 2, i1, jnp.where(lane == 3, i2, 0))))
    gate_ref[...] = jnp.where(lane == 0, w1, jnp.where(lane == 1, w2, 0.0))
    carry_ref[...] = carry_ref[...] + cum[tm - 1:tm, :]
    cnt_ref[...] = carry_ref[...].astype(jnp.int32)


def _router(h, gpre, w_router, *, tm):
    t, d = h.shape
    n_exp = w_router.shape[1]
    wr = jnp.zeros((d, LANES), f32).at[:, :n_exp].set(w_router)
    kern = functools.partial(_router_kernel, tm=tm, d=d, n_exp=n_exp)
    return pl.pallas_call(
        kern,
        grid=(t // tm,),
        in_specs=[
            pl.BlockSpec((tm, d), lambda i: (i, 0)),
            _const_spec((1, d)), _const_spec((d, LANES)),
        ],
        out_specs=[
            pl.BlockSpec((tm, d // 2), lambda i: (i, 0)),
            pl.BlockSpec((tm, LANES), lambda i: (i, 0)),
            pl.BlockSpec((tm, LANES), lambda i: (i, 0)),
            pl.BlockSpec((1, LANES), lambda i: (0, 0)),
        ],
        out_shape=[
            jax.ShapeDtypeStruct((t, d // 2), jnp.uint32),
            jax.ShapeDtypeStruct((t, LANES), jnp.int32),
            jax.ShapeDtypeStruct((t, LANES), f32),
            jax.ShapeDtypeStruct((1, LANES), jnp.int32),
        ],
        scratch_shapes=[pltpu.VMEM((1, LANES), f32)],
        compiler_params=pltpu.CompilerParams(
            dimension_semantics=("arbitrary",), vmem_limit_bytes=VMEM_LIMIT),
        name="moe_router",
    )(h, gpre.reshape(1, d), wr)


def _dispatch_kernel(cnt_ref, start_ref, nv_ref, pos_ref, up_ref, xs_ref, zero_ref, sem, zsem,
                     *, tm, te, n_exp, nt):
    i = pl.program_id(0)

    def row_copy(r, dst):
        return pltpu.make_async_copy(up_ref.at[pl.ds(r, 1)], xs_ref.at[pl.ds(dst, 1)], sem)

    def issue(r, carry):
        for kk in range(TOP_K):
            row_copy(r, pos_ref[TOP_K * r + kk]).start()
        return carry

    lax.fori_loop(0, tm, issue, 0)

    def drain(r, carry):
        for kk in range(TOP_K):
            row_copy(r, pos_ref[TOP_K * r + kk]).wait()
        return carry

    lax.fori_loop(0, tm, drain, 0)

    @pl.when(i == pl.num_programs(0) - 1)
    def _():
        zero_ref[...] = jnp.zeros(zero_ref.shape, zero_ref.dtype)

        def zero_row(dst):
            return pltpu.make_async_copy(zero_ref.at[pl.ds(0, 1)], xs_ref.at[pl.ds(dst, 1)], zsem)

        def zero_tile(ti):
            dst = pl.multiple_of(ti * te, te)
            return pltpu.make_async_copy(zero_ref, xs_ref.at[pl.ds(dst, te)], zsem)

        for ex in range(n_exp):
            lo = start_ref[ex] + cnt_ref[ex]
            hi = start_ref[ex] + ((cnt_ref[ex] + te - 1) // te) * te
            lax.fori_loop(lo, hi, lambda r, c: (zero_row(r).start(), c)[1], 0)
            lax.fori_loop(lo, hi, lambda r, c: (zero_row(r).wait(), c)[1], 0)
        lax.fori_loop(nv_ref[0], nt, lambda ti, c: (zero_tile(ti).start(), c)[1], 0)
        lax.fori_loop(nv_ref[0], nt, lambda ti, c: (zero_tile(ti).wait(), c)[1], 0)


def _dispatch(counts, starts, nvalid, pos_flat, upk, *, tm, te, n_exp, nt):
    t, w = upk.shape
    kern = functools.partial(_dispatch_kernel, tm=tm, te=te, n_exp=n_exp, nt=nt)
    grid_spec = pltpu.PrefetchScalarGridSpec(
        num_scalar_prefetch=3,
        grid=(t // tm,),
        in_specs=[
            pl.BlockSpec((TOP_K * tm,), lambda i, *_: (i,), memory_space=pltpu.SMEM),
            pl.BlockSpec((tm, w), lambda i, *_: (i, 0)),
        ],
        out_specs=pl.BlockSpec(memory_space=pl.ANY),
        scratch_shapes=[pltpu.VMEM((te, w), jnp.uint32), pltpu.SemaphoreType.DMA,
                        pltpu.SemaphoreType.DMA],
    )
    return pl.pallas_call(
        kern,
        grid_spec=grid_spec,
        out_shape=jax.ShapeDtypeStruct((nt * te, w), jnp.uint32),
        compiler_params=pltpu.CompilerParams(
            dimension_semantics=("arbitrary",), vmem_limit_bytes=VMEM_LIMIT),
        name="moe_dispatch",
    )(counts, starts, nvalid, pos_flat, upk)


def _expert_kernel(te_ref, nv_ref, xs_ref, wg_ref, wu_ref, wd_ref, y_ref, a_ref, *, fc, sc):
    i = pl.program_id(0)
    j = pl.program_id(1)

    @pl.when((i >= nv_ref[0]) & (j == 0))
    def _():
        y_ref[...] = jnp.zeros(y_ref.shape, f32)

    @pl.when(i < nv_ref[0])
    def _():
        w = xs_ref[...]
        lo = pltpu.bitcast(w << 16, f32).astype(bf16)
        hi = pltpu.bitcast(w & jnp.uint32(0xFFFF0000), f32).astype(bf16)
        x = jnp.concatenate([lo, hi], axis=1)
        for c in range(fc // sc):
            gate = _dot(x, wg_ref[:, c * sc:(c + 1) * sc])
            up = _dot(x, wu_ref[:, c * sc:(c + 1) * sc])
            a_ref[:, c * sc:(c + 1) * sc] = (_silu(gate) * up).astype(bf16)
        part = _dot(a_ref[...], wd_ref[...])

        @pl.when(j == 0)
        def _():
            y_ref[...] = part

        @pl.when(j > 0)
        def _():
            y_ref[...] = y_ref[...] + part


def _experts(tile_e, nvalid, xs, w_gu, w_down, *, te, fc, sc):
    rows, w = xs.shape
    n_exp, d, f2 = w_gu.shape
    f = f2 // 2
    nf = f // fc
    nt = tile_e.shape[0]

    def jj(i, j, nv):
        return jnp.where(i < nv[0], j, nf - 1)

    grid_spec = pltpu.PrefetchScalarGridSpec(
        num_scalar_prefetch=2,
        grid=(nt, nf),
        in_specs=[
            pl.BlockSpec((te, w), lambda i, j, e, nv: (i, 0)),
            pl.BlockSpec((None, d, fc), lambda i, j, e, nv: (e[i], 0, jj(i, j, nv))),
            pl.BlockSpec((None, d, fc), lambda i, j, e, nv: (e[i], 0, nf + jj(i, j, nv))),
            pl.BlockSpec((None, fc, d), lambda i, j, e, nv: (e[i], jj(i, j, nv), 0)),
        ],
        out_specs=pl.BlockSpec((te, d), lambda i, j, e, nv: (i, 0)),
        scratch_shapes=[pltpu.VMEM((te, fc), bf16)],
    )
    kern = functools.partial(_expert_kernel, fc=fc, sc=sc)
    return pl.pallas_call(
        kern,
        grid_spec=grid_spec,
        out_shape=jax.ShapeDtypeStruct((rows, d), f32),
        compiler_params=pltpu.CompilerParams(
            dimension_semantics=("arbitrary", "arbitrary"), vmem_limit_bytes=VMEM_LIMIT),
        name="moe_experts",
    )(tile_e, nvalid, xs, w_gu, w_gu, w_down)


def _combine_kernel(pos_ref, y_ref, gate_ref, h_ref, gpost_ref, o_ref, buf_ref, sem, *, tm):
    def row_copy(r, kk):
        return pltpu.make_async_copy(y_ref.at[pl.ds(pos_ref[TOP_K * r + kk], 1)],
                                     buf_ref.at[kk, pl.ds(r, 1)], sem)

    def issue(r, carry):
        for kk in range(TOP_K):
            row_copy(r, kk).start()
        return carry

    def drain(r, carry):
        for kk in range(TOP_K):
            row_copy(r, kk).wait()
        return carry

    lax.fori_loop(0, tm, issue, 0)
    lax.fori_loop(0, tm, drain, 0)
    gates = gate_ref[...]
    mix = gates[:, 0:1] * buf_ref[0] + gates[:, 1:2] * buf_ref[1]
    o_ref[...] = h_ref[...] + _rms(mix, gpost_ref[...])


def _combine(pos_flat, y, gates, h, gpost, *, tm):
    t, d = h.shape
    kern = functools.partial(_combine_kernel, tm=tm)
    return pl.pallas_call(
        kern,
        grid=(t // tm,),
        in_specs=[
            pl.BlockSpec((TOP_K * tm,), lambda i: (i,), memory_space=pltpu.SMEM),
            pl.BlockSpec(memory_space=pl.ANY),
            pl.BlockSpec((tm, LANES), lambda i: (i, 0)),
            pl.BlockSpec((tm, d), lambda i: (i, 0)),
            _const_spec((1, d)),
        ],
        out_specs=pl.BlockSpec((tm, d), lambda i: (i, 0)),
        out_shape=jax.ShapeDtypeStruct((t, d), f32),
        scratch_shapes=[pltpu.VMEM((TOP_K, tm, d), f32), pltpu.SemaphoreType.DMA],
        compiler_params=pltpu.CompilerParams(
            dimension_semantics=("arbitrary",), vmem_limit_bytes=VMEM_LIMIT),
        name="moe_combine",
    )(pos_flat, y, gates, h, gpost.reshape(1, d))


def _moe(h, gpre, gpost, w_router, w_gu, w_down, *, tm_route, tm_move, te, fc, sc):
    t, d = h.shape
    n_exp = w_router.shape[1]
    upk, pos, gates, counts = _router(h, gpre, w_router, tm=tm_route)
    counts = counts[0, :n_exp]

    nt = (TOP_K * t) // te + n_exp
    tiles_per = (counts + te - 1) // te
    tile_start = jnp.cumsum(tiles_per) - tiles_per
    nvalid = jnp.sum(tiles_per).reshape(1).astype(jnp.int32)
    idx = jnp.minimum(jnp.arange(nt, dtype=jnp.int32), jnp.maximum(nvalid - 1, 0))
    tile_e = (jnp.sum(idx[:, None] >= tile_start[None, :], axis=1) - 1).astype(jnp.int32)
    row_start = (tile_start * te).astype(jnp.int32)
    pos_flat = (pos[:, :TOP_K] + row_start[pos[:, TOP_K:2 * TOP_K]]).reshape(-1)

    xs = _dispatch(counts, row_start, nvalid, pos_flat, upk, tm=tm_move, te=te, n_exp=n_exp, nt=nt)
    y = _experts(tile_e, nvalid, xs, w_gu, w_down, te=te, fc=fc, sc=sc)
    return _combine(pos_flat, y, gates, h, gpost, tm=tm_move)


def _pick(n, pref):
    return pref if n % pref == 0 else n


def kernel(x, norm_g, conv_w_in, conv_b_in, conv_dw_w, conv_dw_b, conv_ln_g, conv_ln_b, conv_w_out, conv_b_out, hgrn_w_in, hgrn_lower_bounds, hgrn_norm_g, hgrn_w_out, ffn_w_gu, ffn_w_down, moe_router, moe_w_gu, moe_w_down):
    b, l, d = x.shape
    depth = norm_g.shape[0]
    t = b * l
    tm_seq = _pick(l, 512)
    tm_tok = _pick(t, 512)
    h = x
    for i in range(depth):
        j = i // 2
        if i % 2 == 0:
            h = _conv_mixer(h, norm_g[i, 0], norm_g[i, 1], conv_w_in[j].astype(bf16), conv_b_in[j],
                            conv_dw_w[j], conv_dw_b[j], conv_ln_g[j], conv_ln_b[j],
                            conv_w_out[j].astype(bf16), conv_b_out[j], tm=tm_seq)
            f = ffn_w_down.shape[1]
            h = _ffn(h.reshape(t, d), norm_g[i, 2], norm_g[i, 3], ffn_w_gu[j].astype(bf16),
                     ffn_w_down[j].astype(bf16), tm=tm_tok, fc=_pick(f, 256)).reshape(b, l, d)
        else:
            h = _hgrn_mixer(h, norm_g[i, 0], norm_g[i, 1], hgrn_w_in[j].astype(bf16),
                            hgrn_lower_bounds, hgrn_norm_g[j], hgrn_w_out[j].astype(bf16),
                            tm=tm_seq, layer=i)
            f = moe_w_down.shape[2]
            fc = f // 2 if (f // 2) % LANES == 0 else f
            h = _moe(h.reshape(t, d), norm_g[i, 2], norm_g[i, 3], moe_router[j],
                     moe_w_gu[j].astype(bf16), moe_w_down[j].astype(bf16),
                     tm_route=tm_tok, tm_move=_pick(t, 256), te=_pick(t, 512), fc=fc,
                     sc=_pick(fc, 256)).reshape(b, l, d)
    return h
```

```python
import functools

import jax
import jax.numpy as jnp
from jax import lax
from jax.experimental import pallas as pl
from jax.experimental.pallas import tpu as pltpu

EPS = 1e-6
CHUNK = 64
SUB = 16
HEAD = 128
CONV_WIDTH = 31
HALO = 32
CONV_ROWS = 64
LN_ROWS = 32
ROW_PITCH = 2
LANES = 128
SUBLANES = 8
TOP_K = 2
EXP_CLAMP = 80.0
VMEM_LIMIT = 56 * 1024 * 1024

f32 = jnp.float32
bf16 = jnp.bfloat16


def _dot(a, b):
    return jnp.dot(a, b, preferred_element_type=f32)


def _dot_nt(a, b):
    return lax.dot_general(a, b, (((1,), (1,)), ((), ())), preferred_element_type=f32)


def _dot_tn(a, b):
    return lax.dot_general(a, b, (((0,), (0,)), ((), ())), preferred_element_type=f32)


def _rms(x, g):
    return x * lax.rsqrt(jnp.mean(x * x, axis=-1, keepdims=True) + EPS) * g


def _silu(x):
    return x * jax.nn.sigmoid(x)


def _const_spec(shape):
    nd = len(shape)
    return pl.BlockSpec(shape, lambda *_: (0,) * nd, pipeline_mode=pl.Buffered(1))


def _conv_mixer_kernel(x_ref, gpre_ref, gpost_ref, win_ref, bin_ref, dww_ref, dwb_ref,
                       lng_ref, lnb_ref, wout_ref, bout_ref, o_ref, gl_ref, y_ref, z_ref,
                       *, tm, d):
    i = pl.program_id(1)
    nslab = d // LANES

    def rows(start, n):
        return pl.ds(ROW_PITCH * start, n, stride=ROW_PITCH)

    @pl.when(i == 0)
    def _():
        for c in range(nslab):
            gl_ref[c, rows(0, HALO), :] = jnp.zeros((HALO, LANES), f32)

    @pl.when(i > 0)
    def _():
        for c in range(nslab):
            gl_ref[c, rows(0, HALO), :] = gl_ref[c, rows(tm, HALO), :]

    x = x_ref[...]
    u = _rms(x, gpre_ref[...]).astype(bf16)
    a = _dot(u, win_ref[:, :d]) + bin_ref[:, :d]
    gate = _dot(u, win_ref[:, d:]) + bin_ref[:, d:]
    glu = a * jax.nn.sigmoid(gate)
    for c in range(nslab):
        gl_ref[c, rows(HALO, tm), :] = glu[:, c * LANES:(c + 1) * LANES]

    off = HALO - (CONV_WIDTH - 1)
    for c in range(nslab):
        cols = slice(c * LANES, (c + 1) * LANES)
        for rb in range(tm // CONV_ROWS):
            base = rb * CONV_ROWS
            acc = jnp.broadcast_to(dwb_ref[:, cols], (CONV_ROWS, LANES))
            for j in range(CONV_WIDTH):
                acc = acc + dww_ref[j:j + 1, cols] * gl_ref[c, rows(base + off + j, CONV_ROWS), :]
            y_ref[base:base + CONV_ROWS, cols] = acc

    for rb in range(tm // LN_ROWS):
        base = rb * LN_ROWS
        acc = y_ref[base:base + LN_ROWS, :]
        mu = jnp.mean(acc, axis=-1, keepdims=True)
        xc = acc - mu
        y = xc * lax.rsqrt(jnp.mean(xc * xc, axis=-1, keepdims=True) + EPS)
        y = y * lng_ref[...] + lnb_ref[...]
        z_ref[base:base + LN_ROWS, :] = _silu(y).astype(bf16)

    out = _dot(z_ref[...], wout_ref[...]) + bout_ref[...]
    o_ref[...] = x + _rms(out, gpost_ref[...])


def _conv_mixer(x, gpre, gpost, w_in, b_in, dw_w, dw_b, ln_g, ln_b, w_out, b_out, *, tm):
    b, l, d = x.shape
    row = lambda v: v.reshape(1, -1)
    kern = functools.partial(_conv_mixer_kernel, tm=tm, d=d)
    return pl.pallas_call(
        kern,
        grid=(b, l // tm),
        in_specs=[
            pl.BlockSpec((None, tm, d), lambda bi, i: (bi, i, 0)),
            _const_spec((1, d)), _const_spec((1, d)),
            _const_spec((d, 2 * d)), _const_spec((1, 2 * d)),
            _const_spec((CONV_WIDTH, d)), _const_spec((1, d)),
            _const_spec((1, d)), _const_spec((1, d)),
            _const_spec((d, d)), _const_spec((1, d)),
        ],
        out_specs=pl.BlockSpec((None, tm, d), lambda bi, i: (bi, i, 0)),
        out_shape=jax.ShapeDtypeStruct((b, l, d), f32),
        scratch_shapes=[pltpu.VMEM((d // LANES, ROW_PITCH * (tm + HALO), LANES), f32),
                        pltpu.VMEM((tm, d), f32), pltpu.VMEM((tm, d), bf16)],
        compiler_params=pltpu.CompilerParams(
            dimension_semantics=("arbitrary", "arbitrary"), vmem_limit_bytes=VMEM_LIMIT),
        name="conv_mixer",
    )(x, row(gpre), row(gpost), w_in, row(b_in), dw_w, row(dw_b), row(ln_g), row(ln_b),
      w_out, row(b_out))


def _ffn_kernel(h_ref, gpre_ref, gpost_ref, wgu_ref, wd_ref, o_ref, a_ref, *, f, fc):
    x = h_ref[...]
    u = _rms(x, gpre_ref[...]).astype(bf16)
    for c in range(f // fc):
        gate = _dot(u, wgu_ref[:, c * fc:(c + 1) * fc])
        up = _dot(u, wgu_ref[:, f + c * fc:f + (c + 1) * fc])
        a_ref[:, c * fc:(c + 1) * fc] = (_silu(gate) * up).astype(bf16)
    out = _dot(a_ref[...], wd_ref[...])
    o_ref[...] = x + _rms(out, gpost_ref[...])


def _ffn(h, gpre, gpost, w_gu, w_down, *, tm, fc):
    t, d = h.shape
    f = w_down.shape[0]
    kern = functools.partial(_ffn_kernel, f=f, fc=fc)
    return pl.pallas_call(
        kern,
        grid=(t // tm,),
        in_specs=[
            pl.BlockSpec((tm, d), lambda i: (i, 0)),
            _const_spec((1, d)), _const_spec((1, d)),
            _const_spec((d, 2 * f)), _const_spec((f, d)),
        ],
        out_specs=pl.BlockSpec((tm, d), lambda i: (i, 0)),
        out_shape=jax.ShapeDtypeStruct((t, d), f32),
        scratch_shapes=[pltpu.VMEM((tm, f), bf16)],
        compiler_params=pltpu.CompilerParams(
            dimension_semantics=("arbitrary",), vmem_limit_bytes=VMEM_LIMIT),
        name="dense_ffn",
    )(h, gpre.reshape(1, d), gpost.reshape(1, d), w_gu, w_down)


def _hgrn_kernel(h_ref, gpre_ref, gpost_ref, win_ref, lbp_ref, ng_ref, wout_ref, o_ref,
                 q_ref, k_ref, v_ref, g_ref, og_ref, oh_ref, st_ref, *, tm, d, layer):
    i = pl.program_id(1)
    heads = d // HEAD
    nsub = CHUNK // SUB

    @pl.when(i == 0)
    def _():
        st_ref[...] = jnp.zeros(st_ref.shape, f32)

    x = h_ref[...]
    u = _rms(x, gpre_ref[...]).astype(bf16)

    lbp = lbp_ref[...]
    e = jnp.exp(lbp - jnp.max(lbp, axis=0, keepdims=True))
    sm = e / jnp.sum(e, axis=0, keepdims=True)
    lb = jnp.sum(sm[1:layer + 1, :], axis=0, keepdims=True)

    q_ref[...] = _silu(_dot(u, win_ref[:, 0:d]))
    forget = lb + (1.0 - lb) * jax.nn.sigmoid(_dot(u, win_ref[:, d:2 * d]))
    k_ref[...] = 1.0 - forget
    g_ref[...] = jnp.log(forget)
    v_ref[...] = _dot(u, win_ref[:, 2 * d:3 * d]).astype(bf16)
    og_ref[...] = _silu(_dot(u, win_ref[:, 3 * d:4 * d]))

    rr = lax.broadcasted_iota(jnp.int32, (CHUNK, CHUNK), 0)
    cc = lax.broadcasted_iota(jnp.int32, (CHUNK, CHUNK), 1)
    tri = (rr >= cc).astype(bf16)
    diag_mask = (rr >= cc) & ((rr // SUB) == (cc // SUB))
    ng = ng_ref[...]
    zeros_blk = jnp.zeros((SUB, d), f32)

    def block_rows(vals):
        return jnp.concatenate([jnp.broadcast_to(r, (SUB, d)) for r in vals], axis=0)

    def chunk_body(c, carry):
        r0 = pl.multiple_of(c * CHUNK, CHUNK)
        rows = pl.ds(r0, CHUNK)
        g = g_ref[rows, :]
        g_hi = g.astype(bf16)
        g_lo = (g - g_hi.astype(f32)).astype(bf16)
        big_g = _dot(tri, g_hi) + _dot(tri, g_lo)
        q = q_ref[rows, :]
        k = k_ref[rows, :]

        ends = [big_g[SUB * b + SUB - 1:SUB * b + SUB, :] for b in range(nsub)]
        prevs = [jnp.zeros((1, d), f32)] + ends[:-1]
        ref_prev = block_rows(prevs)
        ref_end = block_rows(ends)
        qw = q * jnp.exp(big_g - ref_prev)
        k_end = k * jnp.exp(ref_end - big_g)
        qe = (qw * block_rows([jnp.exp(p) for p in prevs])).astype(bf16)
        k_dec = (k_end * block_rows([jnp.exp(ends[-1] - e) for e in ends])).astype(bf16)
        half = 0.5 * (ref_prev - ref_end)
        q_mid = (q * jnp.exp(jnp.clip(big_g - ref_prev + half, -EXP_CLAMP, EXP_CLAMP))).astype(bf16)
        k_mid = (k * jnp.exp(jnp.clip(ref_end - big_g + half, -EXP_CLAMP, EXP_CLAMP))).astype(bf16)
        q_parts, k_parts = [], []
        for jb in range(nsub - 1):
            qp = [zeros_blk] * (jb + 1)
            for b in range(jb + 1, nsub):
                piece = qw[SUB * b:SUB * (b + 1), :]
                if b > jb + 1:
                    piece = piece * jnp.exp(prevs[b] - ends[jb])
                qp.append(piece)
            q_parts.append(jnp.concatenate(qp, axis=0).astype(bf16))
            kp = [zeros_blk] * nsub
            kp[jb] = k_end[SUB * jb:SUB * (jb + 1), :]
            k_parts.append(jnp.concatenate(kp, axis=0).astype(bf16))
        decay = jnp.exp(ends[-1])

        for hd in range(heads):
            sl = slice(hd * HEAD, (hd + 1) * HEAD)
            v = v_ref[rows, sl]
            st = st_ref[hd]
            s_off = _dot_nt(jnp.concatenate([p[:, sl] for p in q_parts], axis=1),
                            jnp.concatenate([p[:, sl] for p in k_parts], axis=1))
            s_diag = _dot_nt(q_mid[:, sl], k_mid[:, sl])
            scores = (s_off + jnp.where(diag_mask, s_diag, 0.0)).astype(bf16)
            o = _dot_nt(qe[:, sl], st.astype(bf16)) + _dot(scores, v)
            st_ref[hd] = decay[:, sl] * st + _dot_tn(v, k_dec[:, sl])
            o = o * lax.rsqrt(jnp.mean(o * o, axis=-1, keepdims=True) + EPS) * ng
            oh_ref[rows, sl] = (o * og_ref[rows, sl]).astype(bf16)
        return carry

    lax.fori_loop(0, tm // CHUNK, chunk_body, 0, unroll=2)
    out = _dot(oh_ref[...], wout_ref[...])
    o_ref[...] = x + _rms(out, gpost_ref[...])


def _hgrn_mixer(h, gpre, gpost, w_in, lower_bounds, norm_g, w_out, *, tm, layer):
    b, l, d = h.shape
    depth = lower_bounds.shape[0]
    kern = functools.partial(_hgrn_kernel, tm=tm, d=d, layer=layer)
    return pl.pallas_call(
        kern,
        grid=(b, l // tm),
        in_specs=[
            pl.BlockSpec((None, tm, d), lambda bi, i: (bi, i, 0)),
            _const_spec((1, d)), _const_spec((1, d)),
            _const_spec((d, 4 * d)), _const_spec((depth, d)), _const_spec((1, HEAD)),
            _const_spec((d, d)),
        ],
        out_specs=pl.BlockSpec((None, tm, d), lambda bi, i: (bi, i, 0)),
        out_shape=jax.ShapeDtypeStruct((b, l, d), f32),
        scratch_shapes=[pltpu.VMEM((tm, d), f32), pltpu.VMEM((tm, d), f32),
                        pltpu.VMEM((tm, d), bf16), pltpu.VMEM((tm, d), f32),
                        pltpu.VMEM((tm, d), f32), pltpu.VMEM((tm, d), bf16),
                        pltpu.VMEM((d // HEAD, HEAD, HEAD), f32)],
        compiler_params=pltpu.CompilerParams(
            dimension_semantics=("arbitrary", "arbitrary"), vmem_limit_bytes=VMEM_LIMIT),
        name="hgrn_mixer",
    )(h, gpre.reshape(1, d), gpost.reshape(1, d), w_in, lower_bounds, norm_g.reshape(1, HEAD), w_out)


def _router_kernel(h_ref, gpre_ref, wr_ref, up_ref, pos_ref, gate_ref, cnt_ref, carry_ref,
                   *, tm, d, n_exp):
    i = pl.program_id(0)

    @pl.when(i == 0)
    def _():
        carry_ref[...] = jnp.zeros(carry_ref.shape, f32)

    u = _rms(h_ref[...], gpre_ref[...])
    bits = pltpu.bitcast(u.astype(bf16).astype(f32), jnp.uint32)
    half = d // 2
    up_ref[...] = (bits[:, :half] >> 16) | (bits[:, half:] & jnp.uint32(0xFFFF0000))

    logits = jnp.dot(u, wr_ref[...], preferred_element_type=f32, precision=lax.Precision.HIGHEST)
    lane = lax.broadcasted_iota(jnp.int32, (tm, LANES), 1)
    neg = jnp.float32(-jnp.inf)
    lg = jnp.where(lane < n_exp, logits, neg)
    v1 = jnp.max(lg, axis=-1, keepdims=True)
    i1 = jnp.min(jnp.where(lg == v1, lane, LANES), axis=-1, keepdims=True)
    m1 = lane == i1
    lg2 = jnp.where(m1, neg, lg)
    v2 = jnp.max(lg2, axis=-1, keepdims=True)
    i2 = jnp.min(jnp.where(lg2 == v2, lane, LANES), axis=-1, keepdims=True)
    m2 = lane == i2
    dd = jnp.exp(v2 - v1)
    w1 = 1.0 / (1.0 + dd)
    w2 = dd / (1.0 + dd)

    sel = (m1 | m2).astype(f32)
    rr = lax.broadcasted_iota(jnp.int32, (tm, tm), 0)
    cc = lax.broadcasted_iota(jnp.int32, (tm, tm), 1)
    cum = _dot((rr >= cc).astype(bf16), sel.astype(bf16))
    rank = cum - sel + carry_ref[...]
    r1 = jnp.sum(jnp.where(m1, rank, 0.0), axis=-1, keepdims=True).astype(jnp.int32)
    r2 = jnp.sum(jnp.where(m2, rank, 0.0), axis=-1, keepdims=True).astype(jnp.int32)
    pos_ref[...] = jnp.where(lane == 0, r1, jnp.where(lane == 1, r2, jnp.where(
        lane == 2, i1, jnp.where(lane == 3, i2, 0))))
    gate_ref[...] = jnp.where(lane == 0, w1, jnp.where(lane == 1, w2, 0.0))
    carry_ref[...] = carry_ref[...] + cum[tm - 1:tm, :]
    cnt_ref[...] = carry_ref[...].astype(jnp.int32)


def _router(h, gpre, w_router, *, tm):
    t, d = h.shape
    n_exp = w_router.shape[1]
    wr = jnp.zeros((d, LANES), f32).at[:, :n_exp].set(w_router)
    kern = functools.partial(_router_kernel, tm=tm, d=d, n_exp=n_exp)
    return pl.pallas_call(
        kern,
        grid=(t // tm,),
        in_specs=[
            pl.BlockSpec((tm, d), lambda i: (i, 0)),
            _const_spec((1, d)), _const_spec((d, LANES)),
        ],
        out_specs=[
            pl.BlockSpec((tm, d // 2), lambda i: (i, 0)),
            pl.BlockSpec((tm, LANES), lambda i: (i, 0)),
            pl.BlockSpec((tm, LANES), lambda i: (i, 0)),
            pl.BlockSpec((1, LANES), lambda i: (0, 0)),
        ],
        out_shape=[
            jax.ShapeDtypeStruct((t, d // 2), jnp.uint32),
            jax.ShapeDtypeStruct((t, LANES), jnp.int32),
            jax.ShapeDtypeStruct((t, LANES), f32),
            jax.ShapeDtypeStruct((1, LANES), jnp.int32),
        ],
        scratch_shapes=[pltpu.VMEM((1, LANES), f32)],
        compiler_params=pltpu.CompilerParams(
            dimension_semantics=("arbitrary",), vmem_limit_bytes=VMEM_LIMIT),
        name="moe_router",
    )(h, gpre.reshape(1, d), wr)


def _dispatch_kernel(cnt_ref, start_ref, nv_ref, pos_ref, up_ref, xs_ref, zero_ref, sem, zsem,
                     *, tm, te, n_exp, nt):
    i = pl.program_id(0)
    last = pl.num_programs(0) - 1
    base = i * tm

    def issue(r, carry):
        for kk in range(TOP_K):
            pltpu.make_async_copy(up_ref.at[pl.ds(base + r, 1)],
                                  xs_ref.at[pl.ds(pos_ref[TOP_K * r + kk], 1)], sem).start()
        return carry

    def drain_step():
        for _ in range(TOP_K):
            pltpu.make_async_copy(up_ref.at[pl.ds(0, tm)], xs_ref.at[pl.ds(0, tm)], sem).wait()

    lax.fori_loop(0, tm, issue, 0, unroll=8)

    @pl.when(i > 0)
    def _():
        drain_step()

    @pl.when(i == last)
    def _():
        drain_step()
        zero_ref[...] = jnp.zeros(zero_ref.shape, zero_ref.dtype)

        def zero_row(dst):
            return pltpu.make_async_copy(zero_ref.at[pl.ds(0, 1)], xs_ref.at[pl.ds(dst, 1)], zsem)

        def zero_tile(ti):
            dst = pl.multiple_of(ti * te, te)
            return pltpu.make_async_copy(zero_ref, xs_ref.at[pl.ds(dst, te)], zsem)

        for ex in range(n_exp):
            lo = start_ref[ex] + cnt_ref[ex]
            hi = start_ref[ex] + ((cnt_ref[ex] + te - 1) // te) * te
            lax.fori_loop(lo, hi, lambda r, c: (zero_row(r).start(), c)[1], 0)
            lax.fori_loop(lo, hi, lambda r, c: (zero_row(r).wait(), c)[1], 0)
        lax.fori_loop(nv_ref[0], nt, lambda ti, c: (zero_tile(ti).start(), c)[1], 0)
        lax.fori_loop(nv_ref[0], nt, lambda ti, c: (zero_tile(ti).wait(), c)[1], 0)


def _dispatch(counts, starts, nvalid, pos_flat, upk, *, tm, te, n_exp, nt):
    t, w = upk.shape
    kern = functools.partial(_dispatch_kernel, tm=tm, te=te, n_exp=n_exp, nt=nt)
    grid_spec = pltpu.PrefetchScalarGridSpec(
        num_scalar_prefetch=3,
        grid=(t // tm,),
        in_specs=[
            pl.BlockSpec((TOP_K * tm,), lambda i, *_: (i,), memory_space=pltpu.SMEM),
            pl.BlockSpec(memory_space=pl.ANY),
        ],
        out_specs=pl.BlockSpec(memory_space=pl.ANY),
        scratch_shapes=[pltpu.VMEM((te, w), jnp.uint32), pltpu.SemaphoreType.DMA,
                        pltpu.SemaphoreType.DMA],
    )
    return pl.pallas_call(
        kern,
        grid_spec=grid_spec,
        out_shape=jax.ShapeDtypeStruct((nt * te, w), jnp.uint32),
        compiler_params=pltpu.CompilerParams(
            dimension_semantics=("arbitrary",), vmem_limit_bytes=VMEM_LIMIT),
        name="moe_dispatch",
    )(counts, starts, nvalid, pos_flat, upk)


def _expert_kernel(te_ref, nv_ref, xs_ref, wg_ref, wu_ref, wd_ref, y_ref, a_ref, *, fc, sc):
    i = pl.program_id(0)
    j = pl.program_id(1)

    @pl.when((i >= nv_ref[0]) & (j == 0))
    def _():
        y_ref[...] = jnp.zeros(y_ref.shape, f32)

    @pl.when(i < nv_ref[0])
    def _():
        w = xs_ref[...]
        lo = pltpu.bitcast(w << 16, f32).astype(bf16)
        hi = pltpu.bitcast(w & jnp.uint32(0xFFFF0000), f32).astype(bf16)
        x = jnp.concatenate([lo, hi], axis=1)
        for c in range(fc // sc):
            gate = _dot(x, wg_ref[:, c * sc:(c + 1) * sc])
            up = _dot(x, wu_ref[:, c * sc:(c + 1) * sc])
            a_ref[:, c * sc:(c + 1) * sc] = (_silu(gate) * up).astype(bf16)
        part = _dot(a_ref[...], wd_ref[...])

        @pl.when(j == 0)
        def _():
            y_ref[...] = part

        @pl.when(j > 0)
        def _():
            y_ref[...] = y_ref[...] + part


def _experts(tile_e, nvalid, xs, w_gu, w_down, *, te, fc, sc):
    rows, w = xs.shape
    n_exp, d, f2 = w_gu.shape
    f = f2 // 2
    nf = f // fc
    nt = tile_e.shape[0]

    def jj(i, j, nv):
        return jnp.where(i < nv[0], j, nf - 1)

    grid_spec = pltpu.PrefetchScalarGridSpec(
        num_scalar_prefetch=2,
        grid=(nt, nf),
        in_specs=[
            pl.BlockSpec((te, w), lambda i, j, e, nv: (i, 0)),
            pl.BlockSpec((None, d, fc), lambda i, j, e, nv: (e[i], 0, jj(i, j, nv))),
            pl.BlockSpec((None, d, fc), lambda i, j, e, nv: (e[i], 0, nf + jj(i, j, nv))),
            pl.BlockSpec((None, fc, d), lambda i, j, e, nv: (e[i], jj(i, j, nv), 0)),
        ],
        out_specs=pl.BlockSpec((te, d), lambda i, j, e, nv: (i, 0)),
        scratch_shapes=[pltpu.VMEM((te, fc), bf16)],
    )
    kern = functools.partial(_expert_kernel, fc=fc, sc=sc)
    return pl.pallas_call(
        kern,
        grid_spec=grid_spec,
        out_shape=jax.ShapeDtypeStruct((rows, d), f32),
        compiler_params=pltpu.CompilerParams(
            dimension_semantics=("arbitrary", "arbitrary"), vmem_limit_bytes=VMEM_LIMIT),
        name="moe_experts",
    )(tile_e, nvalid, xs, w_gu, w_gu, w_down)


def _combine_kernel(pos_ref, posn_ref, y_ref, gate_ref, h_ref, gpost_ref, o_ref, buf_ref, sem,
                    *, tm):
    i = pl.program_id(0)
    slot = i % 2

    def issue(p_ref, sl):
        def body(r, carry):
            for kk in range(TOP_K):
                pltpu.make_async_copy(y_ref.at[pl.ds(p_ref[TOP_K * r + kk], 1)],
                                      buf_ref.at[sl, kk, pl.ds(r, 1)], sem.at[sl]).start()
            return carry
        lax.fori_loop(0, tm, body, 0, unroll=8)

    @pl.when(i == 0)
    def _():
        issue(pos_ref, 0)

    @pl.when(i + 1 < pl.num_programs(0))
    def _():
        issue(posn_ref, 1 - slot)

    for kk in range(TOP_K):
        pltpu.make_async_copy(y_ref.at[pl.ds(0, tm)], buf_ref.at[slot, kk], sem.at[slot]).wait()
    gates = gate_ref[...]
    mix = gates[:, 0:1] * buf_ref[slot, 0] + gates[:, 1:2] * buf_ref[slot, 1]
    o_ref[...] = h_ref[...] + _rms(mix, gpost_ref[...])


def _combine(pos_flat, y, gates, h, gpost, *, tm):
    t, d = h.shape
    kern = functools.partial(_combine_kernel, tm=tm)
    n_steps = t // tm
    return pl.pallas_call(
        kern,
        grid=(n_steps,),
        in_specs=[
            pl.BlockSpec((TOP_K * tm,), lambda i: (i,), memory_space=pltpu.SMEM),
            pl.BlockSpec((TOP_K * tm,), lambda i: (jnp.minimum(i + 1, n_steps - 1),),
                         memory_space=pltpu.SMEM),
            pl.BlockSpec(memory_space=pl.ANY),
            pl.BlockSpec((tm, LANES), lambda i: (i, 0)),
            pl.BlockSpec((tm, d), lambda i: (i, 0)),
            _const_spec((1, d)),
        ],
        out_specs=pl.BlockSpec((tm, d), lambda i: (i, 0)),
        out_shape=jax.ShapeDtypeStruct((t, d), f32),
        scratch_shapes=[pltpu.VMEM((2, TOP_K, tm, d), f32), pltpu.SemaphoreType.DMA((2,))],
        compiler_params=pltpu.CompilerParams(
            dimension_semantics=("arbitrary",), vmem_limit_bytes=VMEM_LIMIT),
        name="moe_combine",
    )(pos_flat, pos_flat, y, gates, h, gpost.reshape(1, d))


def _moe(h, gpre, gpost, w_router, w_gu, w_down, *, tm_route, tm_move, te, fc, sc):
    t, d = h.shape
    n_exp = w_router.shape[1]
    upk, pos, gates, counts = _router(h, gpre, w_router, tm=tm_route)
    counts = counts[0, :n_exp]

    nt = (TOP_K * t) // te + n_exp
    tiles_per = (counts + te - 1) // te
    tile_start = jnp.cumsum(tiles_per) - tiles_per
    nvalid = jnp.sum(tiles_per).reshape(1).astype(jnp.int32)
    idx = jnp.minimum(jnp.arange(nt, dtype=jnp.int32), jnp.maximum(nvalid - 1, 0))
    tile_e = (jnp.sum(idx[:, None] >= tile_start[None, :], axis=1) - 1).astype(jnp.int32)
    row_start = (tile_start * te).astype(jnp.int32)
    pos_flat = (pos[:, :TOP_K] + row_start[pos[:, TOP_K:2 * TOP_K]]).reshape(-1)

    xs = _dispatch(counts, row_start, nvalid, pos_flat, upk, tm=tm_move, te=te, n_exp=n_exp, nt=nt)
    y = _experts(tile_e, nvalid, xs, w_gu, w_down, te=te, fc=fc, sc=sc)
    return _combine(pos_flat, y, gates, h, gpost, tm=tm_move)


def _pick(n, pref):
    return pref if n % pref == 0 else n


def kernel(x, norm_g, conv_w_in, conv_b_in, conv_dw_w, conv_dw_b, conv_ln_g, conv_ln_b, conv_w_out, conv_b_out, hgrn_w_in, hgrn_lower_bounds, hgrn_norm_g, hgrn_w_out, ffn_w_gu, ffn_w_down, moe_router, moe_w_gu, moe_w_down):
    b, l, d = x.shape
    depth = norm_g.shape[0]
    t = b * l
    tm_seq = _pick(l, 512)
    tm_tok = _pick(t, 512)
    h = x
    for i in range(depth):
        j = i // 2
        if i % 2 == 0:
            h = _conv_mixer(h, norm_g[i, 0], norm_g[i, 1], conv_w_in[j].astype(bf16), conv_b_in[j],
                            conv_dw_w[j], conv_dw_b[j], conv_ln_g[j], conv_ln_b[j],
                            conv_w_out[j].astype(bf16), conv_b_out[j], tm=tm_seq)
            f = ffn_w_down.shape[1]
            h = _ffn(h.reshape(t, d), norm_g[i, 2], norm_g[i, 3], ffn_w_gu[j].astype(bf16),
                     ffn_w_down[j].astype(bf16), tm=tm_tok, fc=_pick(f, 256)).reshape(b, l, d)
        else:
            h = _hgrn_mixer(h, norm_g[i, 0], norm_g[i, 1], hgrn_w_in[j].astype(bf16),
                            hgrn_lower_bounds, hgrn_norm_g[j], hgrn_w_out[j].astype(bf16),
                            tm=tm_seq, layer=i)
            f = moe_w_down.shape[2]
            fc = f // 2 if (f // 2) % LANES == 0 else f
            h = _moe(h.reshape(t, d), norm_g[i, 2], norm_g[i, 3], moe_router[j],
                     moe_w_gu[j].astype(bf16), moe_w_down[j].astype(bf16),
                     tm_route=tm_tok, tm_move=_pick(t, 256), te=_pick(t, 512), fc=fc,
                     sc=_pick(fc, 256)).reshape(b, l, d)
    return h
```

```python
import functools

import jax
import jax.numpy as jnp
from jax import lax
from jax.experimental import pallas as pl
from jax.experimental.pallas import tpu as pltpu

EPS = 1e-6
CHUNK = 64
SUB = 16
HEAD = 128
CONV_WIDTH = 31
HALO = 32
CONV_ROWS = 64
LN_ROWS = 32
ROW_PITCH = 2
LANES = 128
SUBLANES = 8
TOP_K = 2
RING = 3
EXP_CLAMP = 80.0
VMEM_LIMIT = 56 * 1024 * 1024

f32 = jnp.float32
bf16 = jnp.bfloat16


def _dot(a, b):
    return jnp.dot(a, b, preferred_element_type=f32)


def _dot_nt(a, b):
    return lax.dot_general(a, b, (((1,), (1,)), ((), ())), preferred_element_type=f32)


def _dot_tn(a, b):
    return lax.dot_general(a, b, (((0,), (0,)), ((), ())), preferred_element_type=f32)


def _rms(x, g):
    return x * lax.rsqrt(jnp.mean(x * x, axis=-1, keepdims=True) + EPS) * g


def _silu(x):
    return x * jax.nn.sigmoid(x)


def _const_spec(shape):
    nd = len(shape)
    return pl.BlockSpec(shape, lambda *_: (0,) * nd, pipeline_mode=pl.Buffered(1))


def _conv_mixer_kernel(x_ref, gpre_ref, gpost_ref, win_ref, bin_ref, dww_ref, dwb_ref,
                       lng_ref, lnb_ref, wout_ref, bout_ref, o_ref, gl_ref, y_ref, z_ref,
                       *, tm, d):
    i = pl.program_id(1)
    nslab = d // LANES

    def rows(start, n):
        return pl.ds(ROW_PITCH * start, n, stride=ROW_PITCH)

    @pl.when(i == 0)
    def _():
        for c in range(nslab):
            gl_ref[c, rows(0, HALO), :] = jnp.zeros((HALO, LANES), f32)

    @pl.when(i > 0)
    def _():
        for c in range(nslab):
            gl_ref[c, rows(0, HALO), :] = gl_ref[c, rows(tm, HALO), :]

    x = x_ref[...]
    u = _rms(x, gpre_ref[...]).astype(bf16)
    a = _dot(u, win_ref[:, :d]) + bin_ref[:, :d]
    gate = _dot(u, win_ref[:, d:]) + bin_ref[:, d:]
    glu = a * jax.nn.sigmoid(gate)
    for c in range(nslab):
        gl_ref[c, rows(HALO, tm), :] = glu[:, c * LANES:(c + 1) * LANES]

    off = HALO - (CONV_WIDTH - 1)
    for c in range(nslab):
        cols = slice(c * LANES, (c + 1) * LANES)
        for rb in range(tm // CONV_ROWS):
            base = rb * CONV_ROWS
            acc = jnp.broadcast_to(dwb_ref[:, cols], (CONV_ROWS, LANES))
            for j in range(CONV_WIDTH):
                acc = acc + dww_ref[j:j + 1, cols] * gl_ref[c, rows(base + off + j, CONV_ROWS), :]
            y_ref[base:base + CONV_ROWS, cols] = acc

    for rb in range(tm // LN_ROWS):
        base = rb * LN_ROWS
        acc = y_ref[base:base + LN_ROWS, :]
        mu = jnp.mean(acc, axis=-1, keepdims=True)
        xc = acc - mu
        y = xc * lax.rsqrt(jnp.mean(xc * xc, axis=-1, keepdims=True) + EPS)
        y = y * lng_ref[...] + lnb_ref[...]
        z_ref[base:base + LN_ROWS, :] = _silu(y).astype(bf16)

    out = _dot(z_ref[...], wout_ref[...]) + bout_ref[...]
    o_ref[...] = x + _rms(out, gpost_ref[...])


def _conv_mixer(x, gpre, gpost, w_in, b_in, dw_w, dw_b, ln_g, ln_b, w_out, b_out, *, tm):
    b, l, d = x.shape
    row = lambda v: v.reshape(1, -1)
    kern = functools.partial(_conv_mixer_kernel, tm=tm, d=d)
    return pl.pallas_call(
        kern,
        grid=(b, l // tm),
        in_specs=[
            pl.BlockSpec((None, tm, d), lambda bi, i: (bi, i, 0)),
            _const_spec((1, d)), _const_spec((1, d)),
            _const_spec((d, 2 * d)), _const_spec((1, 2 * d)),
            _const_spec((CONV_WIDTH, d)), _const_spec((1, d)),
            _const_spec((1, d)), _const_spec((1, d)),
            _const_spec((d, d)), _const_spec((1, d)),
        ],
        out_specs=pl.BlockSpec((None, tm, d), lambda bi, i: (bi, i, 0)),
        out_shape=jax.ShapeDtypeStruct((b, l, d), f32),
        scratch_shapes=[pltpu.VMEM((d // LANES, ROW_PITCH * (tm + HALO), LANES), f32),
                        pltpu.VMEM((tm, d), f32), pltpu.VMEM((tm, d), bf16)],
        compiler_params=pltpu.CompilerParams(
            dimension_semantics=("arbitrary", "arbitrary"), vmem_limit_bytes=VMEM_LIMIT),
        name="conv_mixer",
    )(x, row(gpre), row(gpost), w_in, row(b_in), dw_w, row(dw_b), row(ln_g), row(ln_b),
      w_out, row(b_out))


def _ffn_kernel(h_ref, gpre_ref, gpost_ref, wgu_ref, wd_ref, o_ref, a_ref, *, f, fc):
    x = h_ref[...]
    u = _rms(x, gpre_ref[...]).astype(bf16)
    for c in range(f // fc):
        gate = _dot(u, wgu_ref[:, c * fc:(c + 1) * fc])
        up = _dot(u, wgu_ref[:, f + c * fc:f + (c + 1) * fc])
        a_ref[:, c * fc:(c + 1) * fc] = (_silu(gate) * up).astype(bf16)
    out = _dot(a_ref[...], wd_ref[...])
    o_ref[...] = x + _rms(out, gpost_ref[...])


def _ffn(h, gpre, gpost, w_gu, w_down, *, tm, fc):
    t, d = h.shape
    f = w_down.shape[0]
    kern = functools.partial(_ffn_kernel, f=f, fc=fc)
    return pl.pallas_call(
        kern,
        grid=(t // tm,),
        in_specs=[
            pl.BlockSpec((tm, d), lambda i: (i, 0)),
            _const_spec((1, d)), _const_spec((1, d)),
            _const_spec((d, 2 * f)), _const_spec((f, d)),
        ],
        out_specs=pl.BlockSpec((tm, d), lambda i: (i, 0)),
        out_shape=jax.ShapeDtypeStruct((t, d), f32),
        scratch_shapes=[pltpu.VMEM((tm, f), bf16)],
        compiler_params=pltpu.CompilerParams(
            dimension_semantics=("arbitrary",), vmem_limit_bytes=VMEM_LIMIT),
        name="dense_ffn",
    )(h, gpre.reshape(1, d), gpost.reshape(1, d), w_gu, w_down)


def _hgrn_kernel(h_ref, gpre_ref, gpost_ref, win_ref, lbp_ref, ng_ref, wout_ref, o_ref,
                 q_ref, k_ref, v_ref, g_ref, og_ref, oh_ref, st_ref, *, tm, d, layer):
    i = pl.program_id(1)
    heads = d // HEAD
    nsub = CHUNK // SUB

    @pl.when(i == 0)
    def _():
        st_ref[...] = jnp.zeros(st_ref.shape, f32)

    x = h_ref[...]
    u = _rms(x, gpre_ref[...]).astype(bf16)

    lbp = lbp_ref[...]
    e = jnp.exp(lbp - jnp.max(lbp, axis=0, keepdims=True))
    sm = e / jnp.sum(e, axis=0, keepdims=True)
    lb = jnp.sum(sm[1:layer + 1, :], axis=0, keepdims=True)

    q_ref[...] = _silu(_dot(u, win_ref[:, 0:d]))
    forget = lb + (1.0 - lb) * jax.nn.sigmoid(_dot(u, win_ref[:, d:2 * d]))
    k_ref[...] = 1.0 - forget
    g_ref[...] = jnp.log(forget)
    v_ref[...] = _dot(u, win_ref[:, 2 * d:3 * d]).astype(bf16)
    og_ref[...] = _silu(_dot(u, win_ref[:, 3 * d:4 * d]))

    rr = lax.broadcasted_iota(jnp.int32, (CHUNK, CHUNK), 0)
    cc = lax.broadcasted_iota(jnp.int32, (CHUNK, CHUNK), 1)
    tri = (rr >= cc).astype(bf16)
    diag_mask = (rr >= cc) & ((rr // SUB) == (cc // SUB))
    ng = ng_ref[...]
    zeros_blk = jnp.zeros((SUB, d), f32)

    def block_rows(vals):
        return jnp.concatenate([jnp.broadcast_to(r, (SUB, d)) for r in vals], axis=0)

    def chunk_body(c, carry):
        r0 = pl.multiple_of(c * CHUNK, CHUNK)
        rows = pl.ds(r0, CHUNK)
        g = g_ref[rows, :]
        g_hi = g.astype(bf16)
        g_lo = (g - g_hi.astype(f32)).astype(bf16)
        big_g = _dot(tri, g_hi) + _dot(tri, g_lo)
        q = q_ref[rows, :]
        k = k_ref[rows, :]

        ends = [big_g[SUB * b + SUB - 1:SUB * b + SUB, :] for b in range(nsub)]
        prevs = [jnp.zeros((1, d), f32)] + ends[:-1]
        ref_prev = block_rows(prevs)
        ref_end = block_rows(ends)
        qw = q * jnp.exp(big_g - ref_prev)
        k_end = k * jnp.exp(ref_end - big_g)
        qe = (qw * block_rows([jnp.exp(p) for p in prevs])).astype(bf16)
        k_dec = (k_end * block_rows([jnp.exp(ends[-1] - e) for e in ends])).astype(bf16)
        half = 0.5 * (ref_prev - ref_end)
        q_mid = (q * jnp.exp(jnp.clip(big_g - ref_prev + half, -EXP_CLAMP, EXP_CLAMP))).astype(bf16)
        k_mid = (k * jnp.exp(jnp.clip(ref_end - big_g + half, -EXP_CLAMP, EXP_CLAMP))).astype(bf16)
        q_parts, k_parts = [], []
        for jb in range(nsub - 1):
            qp = [zeros_blk] * (jb + 1)
            for b in range(jb + 1, nsub):
                piece = qw[SUB * b:SUB * (b + 1), :]
                if b > jb + 1:
                    piece = piece * jnp.exp(prevs[b] - ends[jb])
                qp.append(piece)
            q_parts.append(jnp.concatenate(qp, axis=0).astype(bf16))
            kp = [zeros_blk] * nsub
            kp[jb] = k_end[SUB * jb:SUB * (jb + 1), :]
            k_parts.append(jnp.concatenate(kp, axis=0).astype(bf16))
        decay = jnp.exp(ends[-1])

        for hd in range(heads):
            sl = slice(hd * HEAD, (hd + 1) * HEAD)
            v = v_ref[rows, sl]
            st = st_ref[hd]
            s_off = _dot_nt(jnp.concatenate([p[:, sl] for p in q_parts], axis=1),
                            jnp.concatenate([p[:, sl] for p in k_parts], axis=1))
            s_diag = _dot_nt(q_mid[:, sl], k_mid[:, sl])
            scores = (s_off + jnp.where(diag_mask, s_diag, 0.0)).astype(bf16)
            o = _dot_nt(qe[:, sl], st.astype(bf16)) + _dot(scores, v)
            st_ref[hd] = decay[:, sl] * st + _dot_tn(v, k_dec[:, sl])
            o = o * lax.rsqrt(jnp.mean(o * o, axis=-1, keepdims=True) + EPS) * ng
            oh_ref[rows, sl] = (o * og_ref[rows, sl]).astype(bf16)
        return carry

    lax.fori_loop(0, tm // CHUNK, chunk_body, 0, unroll=2)
    out = _dot(oh_ref[...], wout_ref[...])
    o_ref[...] = x + _rms(out, gpost_ref[...])


def _hgrn_mixer(h, gpre, gpost, w_in, lower_bounds, norm_g, w_out, *, tm, layer):
    b, l, d = h.shape
    depth = lower_bounds.shape[0]
    kern = functools.partial(_hgrn_kernel, tm=tm, d=d, layer=layer)
    return pl.pallas_call(
        kern,
        grid=(b, l // tm),
        in_specs=[
            pl.BlockSpec((None, tm, d), lambda bi, i: (bi, i, 0)),
            _const_spec((1, d)), _const_spec((1, d)),
            _const_spec((d, 4 * d)), _const_spec((depth, d)), _const_spec((1, HEAD)),
            _const_spec((d, d)),
        ],
        out_specs=pl.BlockSpec((None, tm, d), lambda bi, i: (bi, i, 0)),
        out_shape=jax.ShapeDtypeStruct((b, l, d), f32),
        scratch_shapes=[pltpu.VMEM((tm, d), f32), pltpu.VMEM((tm, d), f32),
                        pltpu.VMEM((tm, d), bf16), pltpu.VMEM((tm, d), f32),
                        pltpu.VMEM((tm, d), f32), pltpu.VMEM((tm, d), bf16),
                        pltpu.VMEM((d // HEAD, HEAD, HEAD), f32)],
        compiler_params=pltpu.CompilerParams(
            dimension_semantics=("arbitrary", "arbitrary"), vmem_limit_bytes=VMEM_LIMIT),
        name="hgrn_mixer",
    )(h, gpre.reshape(1, d), gpost.reshape(1, d), w_in, lower_bounds, norm_g.reshape(1, HEAD), w_out)


def _router_kernel(h_ref, gpre_ref, wr_ref, up_ref, pos_ref, gate_ref, cnt_ref, carry_ref,
                   *, tm, d, n_exp):
    i = pl.program_id(0)

    @pl.when(i == 0)
    def _():
        carry_ref[...] = jnp.zeros(carry_ref.shape, f32)

    u = _rms(h_ref[...], gpre_ref[...])
    up_ref[...] = u

    logits = jnp.dot(u, wr_ref[...], preferred_element_type=f32, precision=lax.Precision.HIGHEST)
    lane = lax.broadcasted_iota(jnp.int32, (tm, LANES), 1)
    neg = jnp.float32(-jnp.inf)
    lg = jnp.where(lane < n_exp, logits, neg)
    v1 = jnp.max(lg, axis=-1, keepdims=True)
    i1 = jnp.min(jnp.where(lg == v1, lane, LANES), axis=-1, keepdims=True)
    m1 = lane == i1
    lg2 = jnp.where(m1, neg, lg)
    v2 = jnp.max(lg2, axis=-1, keepdims=True)
    i2 = jnp.min(jnp.where(lg2 == v2, lane, LANES), axis=-1, keepdims=True)
    m2 = lane == i2
    dd = jnp.exp(v2 - v1)
    w1 = 1.0 / (1.0 + dd)
    w2 = dd / (1.0 + dd)

    sel = (m1 | m2).astype(f32)
    rr = lax.broadcasted_iota(jnp.int32, (tm, tm), 0)
    cc = lax.broadcasted_iota(jnp.int32, (tm, tm), 1)
    cum = _dot((rr >= cc).astype(bf16), sel.astype(bf16))
    rank = cum - sel + carry_ref[...]
    r1 = jnp.sum(jnp.where(m1, rank, 0.0), axis=-1, keepdims=True).astype(jnp.int32)
    r2 = jnp.sum(jnp.where(m2, rank, 0.0), axis=-1, keepdims=True).astype(jnp.int32)
    pos_ref[...] = jnp.where(lane == 0, r1, jnp.where(lane == 1, r2, jnp.where(
        lane == 2, i1, jnp.where(lane == 3, i2, 0))))
    gate_ref[...] = jnp.where(lane == 0, w1, jnp.where(lane == 1, w2, 0.0))
    carry_ref[...] = carry_ref[...] + cum[tm - 1:tm, :]
    cnt_ref[...] = carry_ref[...].astype(jnp.int32)


def _router(h, gpre, w_router, *, tm):
    t, d = h.shape
    n_exp = w_router.shape[1]
    wr = jnp.zeros((d, LANES), f32).at[:, :n_exp].set(w_router)
    kern = functools.partial(_router_kernel, tm=tm, d=d, n_exp=n_exp)
    return pl.pallas_call(
        kern,
        grid=(t // tm,),
        in_specs=[
            pl.BlockSpec((tm, d), lambda i: (i, 0)),
            _const_spec((1, d)), _const_spec((d, LANES)),
        ],
        out_specs=[
            pl.BlockSpec((tm, d), lambda i: (i, 0)),
            pl.BlockSpec((tm, LANES), lambda i: (i, 0)),
            pl.BlockSpec((tm, LANES), lambda i: (i, 0)),
            pl.BlockSpec((1, LANES), lambda i: (0, 0)),
        ],
        out_shape=[
            jax.ShapeDtypeStruct((t, d), f32),
            jax.ShapeDtypeStruct((t, LANES), jnp.int32),
            jax.ShapeDtypeStruct((t, LANES), f32),
            jax.ShapeDtypeStruct((1, LANES), jnp.int32),
        ],
        scratch_shapes=[pltpu.VMEM((1, LANES), f32)],
        compiler_params=pltpu.CompilerParams(
            dimension_semantics=("arbitrary",), vmem_limit_bytes=VMEM_LIMIT),
        name="moe_router",
    )(h, gpre.reshape(1, d), wr)


def _dispatch_kernel(cnt_ref, start_ref, nv_ref, pos_ref, up_ref, xs_ref, tile_ref, zero_ref,
                     fsem, sem, zsem, *, tm, te, n_exp, nt):
    i = pl.program_id(0)
    last = pl.num_programs(0) - 1
    slot = i % RING

    def fetch(step, sl):
        src = up_ref.at[pl.ds(pl.multiple_of(step * tm, tm), tm)]
        return pltpu.make_async_copy(src, tile_ref.at[sl], fsem.at[sl])

    def drain_step(sl):
        for _ in range(TOP_K):
            pltpu.make_async_copy(tile_ref.at[sl], xs_ref.at[pl.ds(0, tm)], sem.at[sl]).wait()

    @pl.when(i == 0)
    def _():
        fetch(0, 0).start()

    @pl.when(i < last)
    def _():
        fetch(i + 1, (i + 1) % RING).start()

    fetch(i, slot).wait()

    def issue(r, carry):
        for kk in range(TOP_K):
            pltpu.make_async_copy(tile_ref.at[slot, pl.ds(r, 1)],
                                  xs_ref.at[pl.ds(pos_ref[TOP_K * r + kk], 1)], sem.at[slot]).start()
        return carry

    lax.fori_loop(0, tm, issue, 0, unroll=8)

    @pl.when(i > 0)
    def _():
        drain_step((i + RING - 1) % RING)

    @pl.when(i == last)
    def _():
        drain_step(slot)
        zero_ref[...] = jnp.zeros(zero_ref.shape, zero_ref.dtype)

        def zero_row(dst):
            return pltpu.make_async_copy(zero_ref.at[pl.ds(0, 1)], xs_ref.at[pl.ds(dst, 1)], zsem)

        def zero_tile(ti):
            dst = pl.multiple_of(ti * te, te)
            return pltpu.make_async_copy(zero_ref, xs_ref.at[pl.ds(dst, te)], zsem)

        for ex in range(n_exp):
            lo = start_ref[ex] + cnt_ref[ex]
            hi = start_ref[ex] + ((cnt_ref[ex] + te - 1) // te) * te
            lax.fori_loop(lo, hi, lambda r, c: (zero_row(r).start(), c)[1], 0)
            lax.fori_loop(lo, hi, lambda r, c: (zero_row(r).wait(), c)[1], 0)
        lax.fori_loop(nv_ref[0], nt, lambda ti, c: (zero_tile(ti).start(), c)[1], 0)
        lax.fori_loop(nv_ref[0], nt, lambda ti, c: (zero_tile(ti).wait(), c)[1], 0)


def _dispatch(counts, starts, nvalid, pos_flat, upk, *, tm, te, n_exp, nt):
    t, w = upk.shape
    kern = functools.partial(_dispatch_kernel, tm=tm, te=te, n_exp=n_exp, nt=nt)
    grid_spec = pltpu.PrefetchScalarGridSpec(
        num_scalar_prefetch=3,
        grid=(t // tm,),
        in_specs=[
            pl.BlockSpec((TOP_K * tm,), lambda i, *_: (i,), memory_space=pltpu.SMEM),
            pl.BlockSpec(memory_space=pl.ANY),
        ],
        out_specs=pl.BlockSpec(memory_space=pl.ANY),
        scratch_shapes=[pltpu.VMEM((RING, tm, w), upk.dtype), pltpu.VMEM((te, w), upk.dtype),
                        pltpu.SemaphoreType.DMA((RING,)), pltpu.SemaphoreType.DMA((RING,)),
                        pltpu.SemaphoreType.DMA],
    )
    return pl.pallas_call(
        kern,
        grid_spec=grid_spec,
        out_shape=jax.ShapeDtypeStruct((nt * te, w), upk.dtype),
        compiler_params=pltpu.CompilerParams(
            dimension_semantics=("arbitrary",), vmem_limit_bytes=VMEM_LIMIT),
        name="moe_dispatch",
    )(counts, starts, nvalid, pos_flat, upk)


def _expert_kernel(te_ref, nv_ref, xs_ref, wg_ref, wu_ref, wd_ref, y_ref, a_ref, *, fc, sc):
    i = pl.program_id(0)
    j = pl.program_id(1)

    @pl.when((i >= nv_ref[0]) & (j == 0))
    def _():
        y_ref[...] = jnp.zeros(y_ref.shape, f32)

    @pl.when(i < nv_ref[0])
    def _():
        x = xs_ref[...].astype(bf16)
        for c in range(fc // sc):
            gate = _dot(x, wg_ref[:, c * sc:(c + 1) * sc])
            up = _dot(x, wu_ref[:, c * sc:(c + 1) * sc])
            a_ref[:, c * sc:(c + 1) * sc] = (_silu(gate) * up).astype(bf16)
        part = _dot(a_ref[...], wd_ref[...])

        @pl.when(j == 0)
        def _():
            y_ref[...] = part

        @pl.when(j > 0)
        def _():
            y_ref[...] = y_ref[...] + part


def _experts(tile_e, nvalid, xs, w_gu, w_down, *, te, fc, sc):
    rows, w = xs.shape
    n_exp, d, f2 = w_gu.shape
    f = f2 // 2
    nf = f // fc
    nt = tile_e.shape[0]

    def jj(i, j, nv):
        return jnp.where(i < nv[0], j, nf - 1)

    grid_spec = pltpu.PrefetchScalarGridSpec(
        num_scalar_prefetch=2,
        grid=(nt, nf),
        in_specs=[
            pl.BlockSpec((te, w), lambda i, j, e, nv: (i, 0)),
            pl.BlockSpec((None, d, fc), lambda i, j, e, nv: (e[i], 0, jj(i, j, nv))),
            pl.BlockSpec((None, d, fc), lambda i, j, e, nv: (e[i], 0, nf + jj(i, j, nv))),
            pl.BlockSpec((None, fc, d), lambda i, j, e, nv: (e[i], jj(i, j, nv), 0)),
        ],
        out_specs=pl.BlockSpec((te, d), lambda i, j, e, nv: (i, 0)),
        scratch_shapes=[pltpu.VMEM((te, fc), bf16)],
    )
    kern = functools.partial(_expert_kernel, fc=fc, sc=sc)
    return pl.pallas_call(
        kern,
        grid_spec=grid_spec,
        out_shape=jax.ShapeDtypeStruct((rows, d), f32),
        compiler_params=pltpu.CompilerParams(
            dimension_semantics=("arbitrary", "arbitrary"), vmem_limit_bytes=VMEM_LIMIT),
        name="moe_experts",
    )(tile_e, nvalid, xs, w_gu, w_gu, w_down)


def _combine_kernel(pos_ref, posn_ref, y_ref, gate_ref, h_ref, gpost_ref, o_ref, buf_ref, sem,
                    *, tm):
    i = pl.program_id(0)
    slot = i % 2

    def issue(p_ref, sl):
        def body(r, carry):
            for kk in range(TOP_K):
                pltpu.make_async_copy(y_ref.at[pl.ds(p_ref[TOP_K * r + kk], 1)],
                                      buf_ref.at[sl, kk, pl.ds(r, 1)], sem.at[sl]).start()
            return carry
        lax.fori_loop(0, tm, body, 0, unroll=8)

    @pl.when(i == 0)
    def _():
        issue(pos_ref, 0)

    @pl.when(i + 1 < pl.num_programs(0))
    def _():
        issue(posn_ref, 1 - slot)

    for kk in range(TOP_K):
        pltpu.make_async_copy(y_ref.at[pl.ds(0, tm)], buf_ref.at[slot, kk], sem.at[slot]).wait()
    gates = gate_ref[...]
    mix = gates[:, 0:1] * buf_ref[slot, 0] + gates[:, 1:2] * buf_ref[slot, 1]
    o_ref[...] = h_ref[...] + _rms(mix, gpost_ref[...])


def _combine(pos_flat, y, gates, h, gpost, *, tm):
    t, d = h.shape
    kern = functools.partial(_combine_kernel, tm=tm)
    n_steps = t // tm
    return pl.pallas_call(
        kern,
        grid=(n_steps,),
        in_specs=[
            pl.BlockSpec((TOP_K * tm,), lambda i: (i,), memory_space=pltpu.SMEM),
            pl.BlockSpec((TOP_K * tm,), lambda i: (jnp.minimum(i + 1, n_steps - 1),),
                         memory_space=pltpu.SMEM),
            pl.BlockSpec(memory_space=pl.ANY),
            pl.BlockSpec((tm, LANES), lambda i: (i, 0)),
            pl.BlockSpec((tm, d), lambda i: (i, 0)),
            _const_spec((1, d)),
        ],
        out_specs=pl.BlockSpec((tm, d), lambda i: (i, 0)),
        out_shape=jax.ShapeDtypeStruct((t, d), f32),
        scratch_shapes=[pltpu.VMEM((2, TOP_K, tm, d), f32), pltpu.SemaphoreType.DMA((2,))],
        compiler_params=pltpu.CompilerParams(
            dimension_semantics=("arbitrary",), vmem_limit_bytes=VMEM_LIMIT),
        name="moe_combine",
    )(pos_flat, pos_flat, y, gates, h, gpost.reshape(1, d))


def _moe(h, gpre, gpost, w_router, w_gu, w_down, *, tm_route, tm_move, te, fc, sc):
    t, d = h.shape
    n_exp = w_router.shape[1]
    upk, pos, gates, counts = _router(h, gpre, w_router, tm=tm_route)
    counts = counts[0, :n_exp]

    nt = (TOP_K * t) // te + n_exp
    tiles_per = (counts + te - 1) // te
    tile_start = jnp.cumsum(tiles_per) - tiles_per
    nvalid = jnp.sum(tiles_per).reshape(1).astype(jnp.int32)
    idx = jnp.minimum(jnp.arange(nt, dtype=jnp.int32), jnp.maximum(nvalid - 1, 0))
    tile_e = (jnp.sum(idx[:, None] >= tile_start[None, :], axis=1) - 1).astype(jnp.int32)
    row_start = (tile_start * te).astype(jnp.int32)
    pos_flat = (pos[:, :TOP_K] + row_start[pos[:, TOP_K:2 * TOP_K]]).reshape(-1)

    xs = _dispatch(counts, row_start, nvalid, pos_flat, upk, tm=tm_move, te=te, n_exp=n_exp, nt=nt)
    y = _experts(tile_e, nvalid, xs, w_gu, w_down, te=te, fc=fc, sc=sc)
    return _combine(pos_flat, y, gates, h, gpost, tm=tm_move)


def _pick(n, pref):
    return pref if n % pref == 0 else n


def kernel(x, norm_g, conv_w_in, conv_b_in, conv_dw_w, conv_dw_b, conv_ln_g, conv_ln_b, conv_w_out, conv_b_out, hgrn_w_in, hgrn_lower_bounds, hgrn_norm_g, hgrn_w_out, ffn_w_gu, ffn_w_down, moe_router, moe_w_gu, moe_w_down):
    b, l, d = x.shape
    depth = norm_g.shape[0]
    t = b * l
    tm_seq = _pick(l, 512)
    tm_tok = _pick(t, 512)
    h = x
    for i in range(depth):
        j = i // 2
        if i % 2 == 0:
            h = _conv_mixer(h, norm_g[i, 0], norm_g[i, 1], conv_w_in[j].astype(bf16), conv_b_in[j],
                            conv_dw_w[j], conv_dw_b[j], conv_ln_g[j], conv_ln_b[j],
                            conv_w_out[j].astype(bf16), conv_b_out[j], tm=tm_seq)
            f = ffn_w_down.shape[1]
            h = _ffn(h.reshape(t, d), norm_g[i, 2], norm_g[i, 3], ffn_w_gu[j].astype(bf16),
                     ffn_w_down[j].astype(bf16), tm=tm_tok, fc=_pick(f, 256)).reshape(b, l, d)
        else:
            h = _hgrn_mixer(h, norm_g[i, 0], norm_g[i, 1], hgrn_w_in[j].astype(bf16),
                            hgrn_lower_bounds, hgrn_norm_g[j], hgrn_w_out[j].astype(bf16),
                            tm=tm_seq, layer=i)
            f = moe_w_down.shape[2]
            fc = f // 2 if (f // 2) % LANES == 0 else f
            h = _moe(h.reshape(t, d), norm_g[i, 2], norm_g[i, 3], moe_router[j],
                     moe_w_gu[j].astype(bf16), moe_w_down[j].astype(bf16),
                     tm_route=tm_tok, tm_move=_pick(t, 256), te=_pick(t, 512), fc=fc,
                     sc=_pick(fc, 256)).reshape(b, l, d)
    return h
```

```python
import functools

import jax
import jax.numpy as jnp
from jax import lax
from jax.experimental import pallas as pl
from jax.experimental.pallas import tpu as pltpu

EPS = 1e-6
CHUNK = 64
SUB = 16
HEAD = 128
CONV_WIDTH = 31
HALO = 32
CONV_ROWS = 64
LN_ROWS = 32
ROW_PITCH = 2
LANES = 128
SUBLANES = 8
TOP_K = 2
RING = 3
EXP_CLAMP = 80.0
VMEM_LIMIT = 56 * 1024 * 1024

f32 = jnp.float32
bf16 = jnp.bfloat16


def _dot(a, b):
    return jnp.dot(a, b, preferred_element_type=f32)


def _dot_nt(a, b):
    return lax.dot_general(a, b, (((1,), (1,)), ((), ())), preferred_element_type=f32)


def _dot_tn(a, b):
    return lax.dot_general(a, b, (((0,), (0,)), ((), ())), preferred_element_type=f32)


def _rms(x, g):
    return x * lax.rsqrt(jnp.mean(x * x, axis=-1, keepdims=True) + EPS) * g


def _silu(x):
    return x * jax.nn.sigmoid(x)


def _split_bf16(x):
    hi = x.astype(bf16)
    return hi, (x - hi.astype(f32)).astype(bf16)


def _const_spec(shape):
    nd = len(shape)
    return pl.BlockSpec(shape, lambda *_: (0,) * nd, pipeline_mode=pl.Buffered(1))


def _conv_mixer_kernel(x_ref, gpre_ref, gpost_ref, win_ref, bin_ref, dww_ref, dwb_ref,
                       lng_ref, lnb_ref, wout_ref, bout_ref, o_ref, gl_ref, y_ref, z_ref,
                       *, tm, d):
    i = pl.program_id(1)
    nslab = d // LANES

    def rows(start, n):
        return pl.ds(ROW_PITCH * start, n, stride=ROW_PITCH)

    @pl.when(i == 0)
    def _():
        for c in range(nslab):
            gl_ref[c, rows(0, HALO), :] = jnp.zeros((HALO, LANES), f32)

    @pl.when(i > 0)
    def _():
        for c in range(nslab):
            gl_ref[c, rows(0, HALO), :] = gl_ref[c, rows(tm, HALO), :]

    x = x_ref[...]
    u = _rms(x, gpre_ref[...]).astype(bf16)
    a = _dot(u, win_ref[:, :d]) + bin_ref[:, :d]
    gate = _dot(u, win_ref[:, d:]) + bin_ref[:, d:]
    glu = a * jax.nn.sigmoid(gate)
    for c in range(nslab):
        gl_ref[c, rows(HALO, tm), :] = glu[:, c * LANES:(c + 1) * LANES]

    off = HALO - (CONV_WIDTH - 1)
    for c in range(nslab):
        cols = slice(c * LANES, (c + 1) * LANES)
        for rb in range(tm // CONV_ROWS):
            base = rb * CONV_ROWS
            acc = jnp.broadcast_to(dwb_ref[:, cols], (CONV_ROWS, LANES))
            for j in range(CONV_WIDTH):
                acc = acc + dww_ref[j:j + 1, cols] * gl_ref[c, rows(base + off + j, CONV_ROWS), :]
            y_ref[base:base + CONV_ROWS, cols] = acc

    for rb in range(tm // LN_ROWS):
        base = rb * LN_ROWS
        acc = y_ref[base:base + LN_ROWS, :]
        mu = jnp.mean(acc, axis=-1, keepdims=True)
        xc = acc - mu
        y = xc * lax.rsqrt(jnp.mean(xc * xc, axis=-1, keepdims=True) + EPS)
        y = y * lng_ref[...] + lnb_ref[...]
        z_ref[base:base + LN_ROWS, :] = _silu(y).astype(bf16)

    out = _dot(z_ref[...], wout_ref[...]) + bout_ref[...]
    o_ref[...] = x + _rms(out, gpost_ref[...])


def _conv_mixer(x, gpre, gpost, w_in, b_in, dw_w, dw_b, ln_g, ln_b, w_out, b_out, *, tm):
    b, l, d = x.shape
    row = lambda v: v.reshape(1, -1)
    kern = functools.partial(_conv_mixer_kernel, tm=tm, d=d)
    return pl.pallas_call(
        kern,
        grid=(b, l // tm),
        in_specs=[
            pl.BlockSpec((None, tm, d), lambda bi, i: (bi, i, 0)),
            _const_spec((1, d)), _const_spec((1, d)),
            _const_spec((d, 2 * d)), _const_spec((1, 2 * d)),
            _const_spec((CONV_WIDTH, d)), _const_spec((1, d)),
            _const_spec((1, d)), _const_spec((1, d)),
            _const_spec((d, d)), _const_spec((1, d)),
        ],
        out_specs=pl.BlockSpec((None, tm, d), lambda bi, i: (bi, i, 0)),
        out_shape=jax.ShapeDtypeStruct((b, l, d), f32),
        scratch_shapes=[pltpu.VMEM((d // LANES, ROW_PITCH * (tm + HALO), LANES), f32),
                        pltpu.VMEM((tm, d), f32), pltpu.VMEM((tm, d), bf16)],
        compiler_params=pltpu.CompilerParams(
            dimension_semantics=("arbitrary", "arbitrary"), vmem_limit_bytes=VMEM_LIMIT),
        name="conv_mixer",
    )(x, row(gpre), row(gpost), w_in, row(b_in), dw_w, row(dw_b), row(ln_g), row(ln_b),
      w_out, row(b_out))


def _ffn_kernel(h_ref, gpre_ref, gpost_ref, wgu_ref, wd_ref, o_ref, a_ref, *, f, fc):
    x = h_ref[...]
    u = _rms(x, gpre_ref[...]).astype(bf16)
    for c in range(f // fc):
        gate = _dot(u, wgu_ref[:, c * fc:(c + 1) * fc])
        up = _dot(u, wgu_ref[:, f + c * fc:f + (c + 1) * fc])
        a_ref[:, c * fc:(c + 1) * fc] = (_silu(gate) * up).astype(bf16)
    out = _dot(a_ref[...], wd_ref[...])
    o_ref[...] = x + _rms(out, gpost_ref[...])


def _ffn(h, gpre, gpost, w_gu, w_down, *, tm, fc):
    t, d = h.shape
    f = w_down.shape[0]
    kern = functools.partial(_ffn_kernel, f=f, fc=fc)
    return pl.pallas_call(
        kern,
        grid=(t // tm,),
        in_specs=[
            pl.BlockSpec((tm, d), lambda i: (i, 0)),
            _const_spec((1, d)), _const_spec((1, d)),
            _const_spec((d, 2 * f)), _const_spec((f, d)),
        ],
        out_specs=pl.BlockSpec((tm, d), lambda i: (i, 0)),
        out_shape=jax.ShapeDtypeStruct((t, d), f32),
        scratch_shapes=[pltpu.VMEM((tm, f), bf16)],
        compiler_params=pltpu.CompilerParams(
            dimension_semantics=("arbitrary",), vmem_limit_bytes=VMEM_LIMIT),
        name="dense_ffn",
    )(h, gpre.reshape(1, d), gpost.reshape(1, d), w_gu, w_down)


def _hgrn_kernel(h_ref, gpre_ref, gpost_ref, win_ref, lbp_ref, ng_ref, wout_ref, wgu_ref, wdn_ref,
                 o_ref, wgu_bf_ref, wdn_bf_ref,
                 q_ref, k_ref, v_ref, g_ref, og_ref, oh_ref, st_ref, *, tm, d, layer):
    i = pl.program_id(1)
    heads = d // HEAD
    nsub = CHUNK // SUB

    wgu_bf_ref[...] = wgu_ref[...].astype(bf16)
    wdn_bf_ref[...] = wdn_ref[...].astype(bf16)

    @pl.when(i == 0)
    def _():
        st_ref[...] = jnp.zeros(st_ref.shape, f32)

    x = h_ref[...]
    u = _rms(x, gpre_ref[...]).astype(bf16)

    lbp = lbp_ref[...]
    e = jnp.exp(lbp - jnp.max(lbp, axis=0, keepdims=True))
    sm = e / jnp.sum(e, axis=0, keepdims=True)
    lb = jnp.sum(sm[1:layer + 1, :], axis=0, keepdims=True)

    q_ref[...] = _silu(_dot(u, win_ref[:, 0:d]))
    forget = lb + (1.0 - lb) * jax.nn.sigmoid(_dot(u, win_ref[:, d:2 * d]))
    k_ref[...] = 1.0 - forget
    g_ref[...] = jnp.log(forget)
    v_ref[...] = _dot(u, win_ref[:, 2 * d:3 * d]).astype(bf16)
    og_ref[...] = _silu(_dot(u, win_ref[:, 3 * d:4 * d]))

    rr = lax.broadcasted_iota(jnp.int32, (CHUNK, CHUNK), 0)
    cc = lax.broadcasted_iota(jnp.int32, (CHUNK, CHUNK), 1)
    tri = (rr >= cc).astype(bf16)
    diag_mask = (rr >= cc) & ((rr // SUB) == (cc // SUB))
    ng = ng_ref[...]
    zeros_blk = jnp.zeros((SUB, d), f32)

    def block_rows(vals):
        return jnp.concatenate([jnp.broadcast_to(r, (SUB, d)) for r in vals], axis=0)

    def chunk_body(c, carry):
        r0 = pl.multiple_of(c * CHUNK, CHUNK)
        rows = pl.ds(r0, CHUNK)
        g_hi, g_lo = _split_bf16(g_ref[rows, :])
        big_g = _dot(tri, g_hi) + _dot(tri, g_lo)
        q = q_ref[rows, :]
        k = k_ref[rows, :]

        ends = [big_g[SUB * b + SUB - 1:SUB * b + SUB, :] for b in range(nsub)]
        prevs = [jnp.zeros((1, d), f32)] + ends[:-1]
        ref_prev = block_rows(prevs)
        ref_end = block_rows(ends)
        qw = q * jnp.exp(big_g - ref_prev)
        k_end = k * jnp.exp(ref_end - big_g)
        qe = (qw * block_rows([jnp.exp(p) for p in prevs])).astype(bf16)
        k_dec = (k_end * block_rows([jnp.exp(ends[-1] - e) for e in ends])).astype(bf16)
        half = 0.5 * (ref_prev - ref_end)
        q_mid = (q * jnp.exp(jnp.clip(big_g - ref_prev + half, -EXP_CLAMP, EXP_CLAMP))).astype(bf16)
        k_mid = (k * jnp.exp(jnp.clip(ref_end - big_g + half, -EXP_CLAMP, EXP_CLAMP))).astype(bf16)
        q_parts, k_parts = [], []
        for jb in range(nsub - 1):
            qp = [zeros_blk] * (jb + 1)
            for b in range(jb + 1, nsub):
                piece = qw[SUB * b:SUB * (b + 1), :]
                if b > jb + 1:
                    piece = piece * jnp.exp(prevs[b] - ends[jb])
                qp.append(piece)
            q_parts.append(jnp.concatenate(qp, axis=0).astype(bf16))
            kp = [zeros_blk] * nsub
            kp[jb] = k_end[SUB * jb:SUB * (jb + 1), :]
            k_parts.append(jnp.concatenate(kp, axis=0).astype(bf16))
        decay = jnp.exp(ends[-1])

        for hd in range(heads):
            sl = slice(hd * HEAD, (hd + 1) * HEAD)
            v = v_ref[rows, sl]
            st = st_ref[hd]
            s_off = _dot_nt(jnp.concatenate([p[:, sl] for p in q_parts], axis=1),
                            jnp.concatenate([p[:, sl] for p in k_parts], axis=1))
            s_diag = _dot_nt(q_mid[:, sl], k_mid[:, sl])
            scores = (s_off + jnp.where(diag_mask, s_diag, 0.0)).astype(bf16)
            o = _dot_nt(qe[:, sl], st.astype(bf16)) + _dot(scores, v)
            st_ref[hd] = decay[:, sl] * st + _dot_tn(v, k_dec[:, sl])
            o = o * lax.rsqrt(jnp.mean(o * o, axis=-1, keepdims=True) + EPS) * ng
            oh_ref[rows, sl] = (o * og_ref[rows, sl]).astype(bf16)
        return carry

    lax.fori_loop(0, tm // CHUNK, chunk_body, 0, unroll=2)
    out = _dot(oh_ref[...], wout_ref[...])
    o_ref[...] = x + _rms(out, gpost_ref[...])


def _hgrn_mixer(h, gpre, gpost, w_in, lower_bounds, norm_g, w_out, moe_w_gu, moe_w_down,
                *, tm, layer):
    b, l, d = h.shape
    depth = lower_bounds.shape[0]
    n_tiles = l // tm
    steps = b * n_tiles
    wgu = moe_w_gu.reshape(-1, moe_w_gu.shape[-1])
    wdn = moe_w_down.reshape(-1, moe_w_down.shape[-1])
    gu_rows, dn_rows = wgu.shape[0] // steps, wdn.shape[0] // steps
    assert gu_rows * steps == wgu.shape[0] and dn_rows * steps == wdn.shape[0]
    slice_map = lambda bi, i: (bi * n_tiles + i, 0)
    kern = functools.partial(_hgrn_kernel, tm=tm, d=d, layer=layer)
    out, wgu_bf, wdn_bf = pl.pallas_call(
        kern,
        grid=(b, n_tiles),
        in_specs=[
            pl.BlockSpec((None, tm, d), lambda bi, i: (bi, i, 0)),
            _const_spec((1, d)), _const_spec((1, d)),
            _const_spec((d, 4 * d)), _const_spec((depth, d)), _const_spec((1, HEAD)),
            _const_spec((d, d)),
            pl.BlockSpec((gu_rows, wgu.shape[1]), slice_map),
            pl.BlockSpec((dn_rows, wdn.shape[1]), slice_map),
        ],
        out_specs=[
            pl.BlockSpec((None, tm, d), lambda bi, i: (bi, i, 0)),
            pl.BlockSpec((gu_rows, wgu.shape[1]), slice_map),
            pl.BlockSpec((dn_rows, wdn.shape[1]), slice_map),
        ],
        out_shape=[
            jax.ShapeDtypeStruct((b, l, d), f32),
            jax.ShapeDtypeStruct(wgu.shape, bf16),
            jax.ShapeDtypeStruct(wdn.shape, bf16),
        ],
        scratch_shapes=[pltpu.VMEM((tm, d), f32), pltpu.VMEM((tm, d), f32),
                        pltpu.VMEM((tm, d), bf16), pltpu.VMEM((tm, d), f32),
                        pltpu.VMEM((tm, d), f32), pltpu.VMEM((tm, d), bf16),
                        pltpu.VMEM((d // HEAD, HEAD, HEAD), f32)],
        compiler_params=pltpu.CompilerParams(
            dimension_semantics=("arbitrary", "arbitrary"), vmem_limit_bytes=VMEM_LIMIT),
        name="hgrn_mixer",
    )(h, gpre.reshape(1, d), gpost.reshape(1, d), w_in, lower_bounds, norm_g.reshape(1, HEAD), w_out,
      wgu, wdn)
    return out, wgu_bf.reshape(moe_w_gu.shape), wdn_bf.reshape(moe_w_down.shape)


def _router_kernel(h_ref, gpre_ref, wr_ref, up_ref, pos_ref, gate_ref, cnt_ref, carry_ref,
                   *, tm, d, n_exp):
    i = pl.program_id(0)

    @pl.when(i == 0)
    def _():
        carry_ref[...] = jnp.zeros(carry_ref.shape, f32)

    u = _rms(h_ref[...], gpre_ref[...])
    up_ref[...] = u

    u_hi, u_lo = _split_bf16(u)
    w_hi, w_lo = _split_bf16(wr_ref[...])
    logits = _dot(u_hi, w_hi) + (_dot(u_hi, w_lo) + _dot(u_lo, w_hi))
    lane = lax.broadcasted_iota(jnp.int32, (tm, LANES), 1)
    neg = jnp.float32(-jnp.inf)
    lg = jnp.where(lane < n_exp, logits, neg)
    v1 = jnp.max(lg, axis=-1, keepdims=True)
    i1 = jnp.min(jnp.where(lg == v1, lane, LANES), axis=-1, keepdims=True)
    m1 = lane == i1
    lg2 = jnp.where(m1, neg, lg)
    v2 = jnp.max(lg2, axis=-1, keepdims=True)
    i2 = jnp.min(jnp.where(lg2 == v2, lane, LANES), axis=-1, keepdims=True)
    m2 = lane == i2
    dd = jnp.exp(v2 - v1)
    w1 = 1.0 / (1.0 + dd)
    w2 = dd / (1.0 + dd)

    sel = (m1 | m2).astype(f32)
    rr = lax.broadcasted_iota(jnp.int32, (tm, tm), 0)
    cc = lax.broadcasted_iota(jnp.int32, (tm, tm), 1)
    cum = _dot((rr >= cc).astype(bf16), sel.astype(bf16))
    rank = cum - sel + carry_ref[...]
    r1 = jnp.sum(jnp.where(m1, rank, 0.0), axis=-1, keepdims=True).astype(jnp.int32)
    r2 = jnp.sum(jnp.where(m2, rank, 0.0), axis=-1, keepdims=True).astype(jnp.int32)
    pos_ref[...] = jnp.where(lane == 0, r1, jnp.where(lane == 1, r2, jnp.where(
        lane == 2, i1, jnp.where(lane == 3, i2, 0))))
    gate_ref[...] = jnp.where(lane == 0, w1, jnp.where(lane == 1, w2, 0.0))
    carry_ref[...] = carry_ref[...] + cum[tm - 1:tm, :]
    cnt_ref[...] = carry_ref[...].astype(jnp.int32)


def _router(h, gpre, w_router, *, tm):
    t, d = h.shape
    n_exp = w_router.shape[1]
    wr = jnp.zeros((d, LANES), f32).at[:, :n_exp].set(w_router)
    kern = functools.partial(_router_kernel, tm=tm, d=d, n_exp=n_exp)
    return pl.pallas_call(
        kern,
        grid=(t // tm,),
        in_specs=[
            pl.BlockSpec((tm, d), lambda i: (i, 0)),
            _const_spec((1, d)), _const_spec((d, LANES)),
        ],
        out_specs=[
            pl.BlockSpec((tm, d), lambda i: (i, 0)),
            pl.BlockSpec((tm, LANES), lambda i: (i, 0)),
            pl.BlockSpec((tm, LANES), lambda i: (i, 0)),
            pl.BlockSpec((1, LANES), lambda i: (0, 0)),
        ],
        out_shape=[
            jax.ShapeDtypeStruct((t, d), f32),
            jax.ShapeDtypeStruct((t, LANES), jnp.int32),
            jax.ShapeDtypeStruct((t, LANES), f32),
            jax.ShapeDtypeStruct((1, LANES), jnp.int32),
        ],
        scratch_shapes=[pltpu.VMEM((1, LANES), f32)],
        compiler_params=pltpu.CompilerParams(
            dimension_semantics=("arbitrary",), vmem_limit_bytes=VMEM_LIMIT),
        name="moe_router",
    )(h, gpre.reshape(1, d), wr)


def _dispatch_kernel(cnt_ref, start_ref, nv_ref, pos_ref, up_ref, xs_ref, tile_ref, zero_ref,
                     fsem, sem, zsem, *, tm, te, n_exp, nt):
    i = pl.program_id(0)
    last = pl.num_programs(0) - 1
    slot = i % RING

    def fetch(step, sl):
        src = up_ref.at[pl.ds(pl.multiple_of(step * tm, tm), tm)]
        return pltpu.make_async_copy(src, tile_ref.at[sl], fsem.at[sl])

    def drain_step(sl):
        for _ in range(TOP_K):
            pltpu.make_async_copy(tile_ref.at[sl], xs_ref.at[pl.ds(0, tm)], sem.at[sl]).wait()

    @pl.when(i == 0)
    def _():
        fetch(0, 0).start()

    @pl.when(i < last)
    def _():
        fetch(i + 1, (i + 1) % RING).start()

    fetch(i, slot).wait()

    def issue(r, carry):
        for kk in range(TOP_K):
            pltpu.make_async_copy(tile_ref.at[slot, pl.ds(r, 1)],
                                  xs_ref.at[pl.ds(pos_ref[TOP_K * r + kk], 1)], sem.at[slot]).start()
        return carry

    lax.fori_loop(0, tm, issue, 0, unroll=8)

    @pl.when(i > 0)
    def _():
        drain_step((i + RING - 1) % RING)

    @pl.when(i == last)
    def _():
        drain_step(slot)
        zero_ref[...] = jnp.zeros(zero_ref.shape, zero_ref.dtype)

        def zero_row(dst):
            return pltpu.make_async_copy(zero_ref.at[pl.ds(0, 1)], xs_ref.at[pl.ds(dst, 1)], zsem)

        def zero_tile(ti):
            dst = pl.multiple_of(ti * te, te)
            return pltpu.make_async_copy(zero_ref, xs_ref.at[pl.ds(dst, te)], zsem)

        for ex in range(n_exp):
            lo = start_ref[ex] + cnt_ref[ex]
            hi = start_ref[ex] + ((cnt_ref[ex] + te - 1) // te) * te
            lax.fori_loop(lo, hi, lambda r, c: (zero_row(r).start(), c)[1], 0)
            lax.fori_loop(lo, hi, lambda r, c: (zero_row(r).wait(), c)[1], 0)
        lax.fori_loop(nv_ref[0], nt, lambda ti, c: (zero_tile(ti).start(), c)[1], 0)
        lax.fori_loop(nv_ref[0], nt, lambda ti, c: (zero_tile(ti).wait(), c)[1], 0)


def _dispatch(counts, starts, nvalid, pos_flat, upk, *, tm, te, n_exp, nt):
    t, w = upk.shape
    kern = functools.partial(_dispatch_kernel, tm=tm, te=te, n_exp=n_exp, nt=nt)
    grid_spec = pltpu.PrefetchScalarGridSpec(
        num_scalar_prefetch=3,
        grid=(t // tm,),
        in_specs=[
            pl.BlockSpec((TOP_K * tm,), lambda i, *_: (i,), memory_space=pltpu.SMEM),
            pl.BlockSpec(memory_space=pl.ANY),
        ],
        out_specs=pl.BlockSpec(memory_space=pl.ANY),
        scratch_shapes=[pltpu.VMEM((RING, tm, w), upk.dtype), pltpu.VMEM((te, w), upk.dtype),
                        pltpu.SemaphoreType.DMA((RING,)), pltpu.SemaphoreType.DMA((RING,)),
                        pltpu.SemaphoreType.DMA],
    )
    return pl.pallas_call(
        kern,
        grid_spec=grid_spec,
        out_shape=jax.ShapeDtypeStruct((nt * te, w), upk.dtype),
        compiler_params=pltpu.CompilerParams(
            dimension_semantics=("arbitrary",), vmem_limit_bytes=VMEM_LIMIT),
        name="moe_dispatch",
    )(counts, starts, nvalid, pos_flat, upk)


def _expert_kernel(te_ref, nv_ref, xs_ref, wg_ref, wu_ref, wd_ref, y_ref, a_ref, *, fc, sc):
    i = pl.program_id(0)
    j = pl.program_id(1)

    @pl.when((i >= nv_ref[0]) & (j == 0))
    def _():
        y_ref[...] = jnp.zeros(y_ref.shape, f32)

    @pl.when(i < nv_ref[0])
    def _():
        x = xs_ref[...].astype(bf16)
        for c in range(fc // sc):
            gate = _dot(x, wg_ref[:, c * sc:(c + 1) * sc])
            up = _dot(x, wu_ref[:, c * sc:(c + 1) * sc])
            a_ref[:, c * sc:(c + 1) * sc] = (_silu(gate) * up).astype(bf16)
        part = _dot(a_ref[...], wd_ref[...])

        @pl.when(j == 0)
        def _():
            y_ref[...] = part

        @pl.when(j > 0)
        def _():
            y_ref[...] = y_ref[...] + part


def _experts(tile_e, nvalid, xs, w_gu, w_down, *, te, fc, sc):
    rows, w = xs.shape
    n_exp, d, f2 = w_gu.shape
    f = f2 // 2
    nf = f // fc
    nt = tile_e.shape[0]

    def jj(i, j, nv):
        return jnp.where(i < nv[0], j, nf - 1)

    grid_spec = pltpu.PrefetchScalarGridSpec(
        num_scalar_prefetch=2,
        grid=(nt, nf),
        in_specs=[
            pl.BlockSpec((te, w), lambda i, j, e, nv: (i, 0)),
            pl.BlockSpec((None, d, fc), lambda i, j, e, nv: (e[i], 0, jj(i, j, nv))),
            pl.BlockSpec((None, d, fc), lambda i, j, e, nv: (e[i], 0, nf + jj(i, j, nv))),
            pl.BlockSpec((None, fc, d), lambda i, j, e, nv: (e[i], jj(i, j, nv), 0)),
        ],
        out_specs=pl.BlockSpec((te, d), lambda i, j, e, nv: (i, 0)),
        scratch_shapes=[pltpu.VMEM((te, fc), bf16)],
    )
    kern = functools.partial(_expert_kernel, fc=fc, sc=sc)
    return pl.pallas_call(
        kern,
        grid_spec=grid_spec,
        out_shape=jax.ShapeDtypeStruct((rows, d), f32),
        compiler_params=pltpu.CompilerParams(
            dimension_semantics=("arbitrary", "arbitrary"), vmem_limit_bytes=VMEM_LIMIT),
        name="moe_experts",
    )(tile_e, nvalid, xs, w_gu, w_gu, w_down)


def _combine_kernel(pos_ref, posn_ref, y_ref, gate_ref, h_ref, gpost_ref, o_ref, buf_ref, sem,
                    *, tm):
    i = pl.program_id(0)
    slot = i % 2

    def issue(p_ref, sl):
        def body(r, carry):
            for kk in range(TOP_K):
                pltpu.make_async_copy(y_ref.at[pl.ds(p_ref[TOP_K * r + kk], 1)],
                                      buf_ref.at[sl, kk, pl.ds(r, 1)], sem.at[sl]).start()
            return carry
        lax.fori_loop(0, tm, body, 0, unroll=8)

    @pl.when(i == 0)
    def _():
        issue(pos_ref, 0)

    @pl.when(i + 1 < pl.num_programs(0))
    def _():
        issue(posn_ref, 1 - slot)

    for kk in range(TOP_K):
        pltpu.make_async_copy(y_ref.at[pl.ds(0, tm)], buf_ref.at[slot, kk], sem.at[slot]).wait()
    gates = gate_ref[...]
    mix = gates[:, 0:1] * buf_ref[slot, 0] + gates[:, 1:2] * buf_ref[slot, 1]
    o_ref[...] = h_ref[...] + _rms(mix, gpost_ref[...])


def _combine(pos_flat, y, gates, h, gpost, *, tm):
    t, d = h.shape
    kern = functools.partial(_combine_kernel, tm=tm)
    n_steps = t // tm
    return pl.pallas_call(
        kern,
        grid=(n_steps,),
        in_specs=[
            pl.BlockSpec((TOP_K * tm,), lambda i: (i,), memory_space=pltpu.SMEM),
            pl.BlockSpec((TOP_K * tm,), lambda i: (jnp.minimum(i + 1, n_steps - 1),),
                         memory_space=pltpu.SMEM),
            pl.BlockSpec(memory_space=pl.ANY),
            pl.BlockSpec((tm, LANES), lambda i: (i, 0)),
            pl.BlockSpec((tm, d), lambda i: (i, 0)),
            _const_spec((1, d)),
        ],
        out_specs=pl.BlockSpec((tm, d), lambda i: (i, 0)),
        out_shape=jax.ShapeDtypeStruct((t, d), f32),
        scratch_shapes=[pltpu.VMEM((2, TOP_K, tm, d), f32), pltpu.SemaphoreType.DMA((2,))],
        compiler_params=pltpu.CompilerParams(
            dimension_semantics=("arbitrary",), vmem_limit_bytes=VMEM_LIMIT),
        name="moe_combine",
    )(pos_flat, pos_flat, y, gates, h, gpost.reshape(1, d))


def _moe(h, gpre, gpost, w_router, w_gu, w_down, *, tm_route, tm_move, te, fc, sc):
    t, d = h.shape
    n_exp = w_router.shape[1]
    upk, pos, gates, counts = _router(h, gpre, w_router, tm=tm_route)
    counts = counts[0, :n_exp]

    nt = (TOP_K * t) // te + n_exp
    tiles_per = (counts + te - 1) // te
    tile_start = jnp.cumsum(tiles_per) - tiles_per
    nvalid = jnp.sum(tiles_per).reshape(1).astype(jnp.int32)
    idx = jnp.minimum(jnp.arange(nt, dtype=jnp.int32), jnp.maximum(nvalid - 1, 0))
    tile_e = (jnp.sum(idx[:, None] >= tile_start[None, :], axis=1) - 1).astype(jnp.int32)
    row_start = (tile_start * te).astype(jnp.int32)
    pos_flat = (pos[:, :TOP_K] + row_start[pos[:, TOP_K:2 * TOP_K]]).reshape(-1)

    xs = _dispatch(counts, row_start, nvalid, pos_flat, upk, tm=tm_move, te=te, n_exp=n_exp, nt=nt)
    y = _experts(tile_e, nvalid, xs, w_gu, w_down, te=te, fc=fc, sc=sc)
    return _combine(pos_flat, y, gates, h, gpost, tm=tm_move)


def _pick(n, pref):
    return pref if n % pref == 0 else n


def kernel(x, norm_g, conv_w_in, conv_b_in, conv_dw_w, conv_dw_b, conv_ln_g, conv_ln_b, conv_w_out, conv_b_out, hgrn_w_in, hgrn_lower_bounds, hgrn_norm_g, hgrn_w_out, ffn_w_gu, ffn_w_down, moe_router, moe_w_gu, moe_w_down):
    b, l, d = x.shape
    depth = norm_g.shape[0]
    t = b * l
    tm_seq = _pick(l, 512)
    tm_tok = _pick(t, 512)
    h = x
    for i in range(depth):
        j = i // 2
        if i % 2 == 0:
            h = _conv_mixer(h, norm_g[i, 0], norm_g[i, 1], conv_w_in[j].astype(bf16), conv_b_in[j],
                            conv_dw_w[j], conv_dw_b[j], conv_ln_g[j], conv_ln_b[j],
                            conv_w_out[j].astype(bf16), conv_b_out[j], tm=tm_seq)
            f = ffn_w_down.shape[1]
            h = _ffn(h.reshape(t, d), norm_g[i, 2], norm_g[i, 3], ffn_w_gu[j].astype(bf16),
                     ffn_w_down[j].astype(bf16), tm=tm_tok, fc=_pick(f, 256)).reshape(b, l, d)
        else:
            h, w_gu_bf, w_down_bf = _hgrn_mixer(
                h, norm_g[i, 0], norm_g[i, 1], hgrn_w_in[j].astype(bf16), hgrn_lower_bounds,
                hgrn_norm_g[j], hgrn_w_out[j].astype(bf16), moe_w_gu[j], moe_w_down[j],
                tm=tm_seq, layer=i)
            f = moe_w_down.shape[2]
            fc = f // 2 if (f // 2) % LANES == 0 else f
            h = _moe(h.reshape(t, d), norm_g[i, 2], norm_g[i, 3], moe_router[j],
                     w_gu_bf, w_down_bf,
                     tm_route=tm_tok, tm_move=_pick(t, 256), te=_pick(t, 512), fc=fc,
                     sc=_pick(fc, 256)).reshape(b, l, d)
    return h
```

```python
import functools

import jax
import jax.numpy as jnp
from jax import lax
from jax.experimental import pallas as pl
from jax.experimental.pallas import tpu as pltpu

EPS = 1e-6
CHUNK = 64
SUB = 16
HEAD = 128
CONV_WIDTH = 31
HALO = 32
CONV_ROWS = 64
LN_ROWS = 32
ROW_PITCH = 2
LANES = 128
SUBLANES = 8
TOP_K = 2
RING = 3
EXP_CLAMP = 80.0
VMEM_LIMIT = 56 * 1024 * 1024

f32 = jnp.float32
bf16 = jnp.bfloat16


def _dot(a, b):
    return jnp.dot(a, b, preferred_element_type=f32)


def _dot_nt(a, b):
    return lax.dot_general(a, b, (((1,), (1,)), ((), ())), preferred_element_type=f32)


def _dot_tn(a, b):
    return lax.dot_general(a, b, (((0,), (0,)), ((), ())), preferred_element_type=f32)


def _rms(x, g):
    return x * lax.rsqrt(jnp.mean(x * x, axis=-1, keepdims=True) + EPS) * g


def _silu(x):
    return x * jax.nn.sigmoid(x)


def _row_slabs(d):
    return (d // LANES, LANES)


def _split_bf16(x):
    hi = x.astype(bf16)
    return hi, (x - hi.astype(f32)).astype(bf16)


def _const_spec(shape):
    nd = len(shape)
    return pl.BlockSpec(shape, lambda *_: (0,) * nd, pipeline_mode=pl.Buffered(1))


def _conv_mixer_kernel(x_ref, gpre_ref, gpost_ref, win_ref, bin_ref, dww_ref, dwb_ref,
                       lng_ref, lnb_ref, wout_ref, bout_ref, o_ref, gl_ref, y_ref, z_ref,
                       *, tm, d):
    i = pl.program_id(1)
    nslab = d // LANES

    def rows(start, n):
        return pl.ds(ROW_PITCH * start, n, stride=ROW_PITCH)

    @pl.when(i == 0)
    def _():
        for c in range(nslab):
            gl_ref[c, rows(0, HALO), :] = jnp.zeros((HALO, LANES), f32)

    @pl.when(i > 0)
    def _():
        for c in range(nslab):
            gl_ref[c, rows(0, HALO), :] = gl_ref[c, rows(tm, HALO), :]

    x = x_ref[...]
    u = _rms(x, gpre_ref[...]).astype(bf16)
    a = _dot(u, win_ref[:, :d]) + bin_ref[:, :d]
    gate = _dot(u, win_ref[:, d:]) + bin_ref[:, d:]
    glu = a * jax.nn.sigmoid(gate)
    for c in range(nslab):
        gl_ref[c, rows(HALO, tm), :] = glu[:, c * LANES:(c + 1) * LANES]

    off = HALO - (CONV_WIDTH - 1)
    for c in range(nslab):
        cols = slice(c * LANES, (c + 1) * LANES)
        for rb in range(tm // CONV_ROWS):
            base = rb * CONV_ROWS
            acc = jnp.broadcast_to(dwb_ref[:, cols], (CONV_ROWS, LANES))
            for j in range(CONV_WIDTH):
                acc = acc + dww_ref[j:j + 1, cols] * gl_ref[c, rows(base + off + j, CONV_ROWS), :]
            y_ref[base:base + CONV_ROWS, cols] = acc

    for rb in range(tm // LN_ROWS):
        base = rb * LN_ROWS
        acc = y_ref[base:base + LN_ROWS, :]
        mu = jnp.mean(acc, axis=-1, keepdims=True)
        xc = acc - mu
        y = xc * lax.rsqrt(jnp.mean(xc * xc, axis=-1, keepdims=True) + EPS)
        y = y * lng_ref[...] + lnb_ref[...]
        z_ref[base:base + LN_ROWS, :] = _silu(y).astype(bf16)

    out = _dot(z_ref[...], wout_ref[...]) + bout_ref[...]
    o_ref[...] = x + _rms(out, gpost_ref[...])


def _conv_mixer(x, gpre, gpost, w_in, b_in, dw_w, dw_b, ln_g, ln_b, w_out, b_out, *, tm):
    b, l, d = x.shape
    row = lambda v: v.reshape(1, -1)
    kern = functools.partial(_conv_mixer_kernel, tm=tm, d=d)
    return pl.pallas_call(
        kern,
        grid=(b, l // tm),
        in_specs=[
            pl.BlockSpec((None, tm, d), lambda bi, i: (bi, i, 0)),
            _const_spec((1, d)), _const_spec((1, d)),
            _const_spec((d, 2 * d)), _const_spec((1, 2 * d)),
            _const_spec((CONV_WIDTH, d)), _const_spec((1, d)),
            _const_spec((1, d)), _const_spec((1, d)),
            _const_spec((d, d)), _const_spec((1, d)),
        ],
        out_specs=pl.BlockSpec((None, tm, d), lambda bi, i: (bi, i, 0)),
        out_shape=jax.ShapeDtypeStruct((b, l, d), f32),
        scratch_shapes=[pltpu.VMEM((d // LANES, ROW_PITCH * (tm + HALO), LANES), f32),
                        pltpu.VMEM((tm, d), f32), pltpu.VMEM((tm, d), bf16)],
        compiler_params=pltpu.CompilerParams(
            dimension_semantics=("arbitrary", "arbitrary"), vmem_limit_bytes=VMEM_LIMIT),
        name="conv_mixer",
    )(x, row(gpre), row(gpost), w_in, row(b_in), dw_w, row(dw_b), row(ln_g), row(ln_b),
      w_out, row(b_out))


def _ffn_kernel(h_ref, gpre_ref, gpost_ref, wgu_ref, wd_ref, o_ref, a_ref, *, f, fc):
    x = h_ref[...]
    u = _rms(x, gpre_ref[...]).astype(bf16)
    for c in range(f // fc):
        gate = _dot(u, wgu_ref[:, c * fc:(c + 1) * fc])
        up = _dot(u, wgu_ref[:, f + c * fc:f + (c + 1) * fc])
        a_ref[:, c * fc:(c + 1) * fc] = (_silu(gate) * up).astype(bf16)
    out = _dot(a_ref[...], wd_ref[...])
    o_ref[...] = x + _rms(out, gpost_ref[...])


def _ffn(h, gpre, gpost, w_gu, w_down, *, tm, fc):
    t, d = h.shape
    f = w_down.shape[0]
    kern = functools.partial(_ffn_kernel, f=f, fc=fc)
    return pl.pallas_call(
        kern,
        grid=(t // tm,),
        in_specs=[
            pl.BlockSpec((tm, d), lambda i: (i, 0)),
            _const_spec((1, d)), _const_spec((1, d)),
            _const_spec((d, 2 * f)), _const_spec((f, d)),
        ],
        out_specs=pl.BlockSpec((tm, d), lambda i: (i, 0)),
        out_shape=jax.ShapeDtypeStruct((t, d), f32),
        scratch_shapes=[pltpu.VMEM((tm, f), bf16)],
        compiler_params=pltpu.CompilerParams(
            dimension_semantics=("arbitrary",), vmem_limit_bytes=VMEM_LIMIT),
        name="dense_ffn",
    )(h, gpre.reshape(1, d), gpost.reshape(1, d), w_gu, w_down)


def _hgrn_kernel(h_ref, gpre_ref, gpost_ref, win_ref, lbp_ref, ng_ref, wout_ref, wgu_ref, wdn_ref,
                 o_ref, wgu_bf_ref, wdn_bf_ref,
                 q_ref, k_ref, v_ref, g_ref, og_ref, oh_ref, st_ref, *, tm, d, layer):
    i = pl.program_id(1)
    heads = d // HEAD
    nsub = CHUNK // SUB

    wgu_bf_ref[...] = wgu_ref[...].astype(bf16)
    wdn_bf_ref[...] = wdn_ref[...].astype(bf16)

    @pl.when(i == 0)
    def _():
        st_ref[...] = jnp.zeros(st_ref.shape, f32)

    x = h_ref[...]
    u = _rms(x, gpre_ref[...]).astype(bf16)

    lbp = lbp_ref[...]
    e = jnp.exp(lbp - jnp.max(lbp, axis=0, keepdims=True))
    sm = e / jnp.sum(e, axis=0, keepdims=True)
    lb = jnp.sum(sm[1:layer + 1, :], axis=0, keepdims=True)

    q_ref[...] = _silu(_dot(u, win_ref[:, 0:d]))
    forget = lb + (1.0 - lb) * jax.nn.sigmoid(_dot(u, win_ref[:, d:2 * d]))
    k_ref[...] = 1.0 - forget
    g_ref[...] = jnp.log(forget)
    v_ref[...] = _dot(u, win_ref[:, 2 * d:3 * d]).astype(bf16)
    og_ref[...] = _silu(_dot(u, win_ref[:, 3 * d:4 * d]))

    rr = lax.broadcasted_iota(jnp.int32, (CHUNK, CHUNK), 0)
    cc = lax.broadcasted_iota(jnp.int32, (CHUNK, CHUNK), 1)
    tri = (rr >= cc).astype(bf16)
    diag_mask = (rr >= cc) & ((rr // SUB) == (cc // SUB))
    ng = ng_ref[...]
    zeros_blk = jnp.zeros((SUB, d), f32)

    def block_rows(vals):
        return jnp.concatenate([jnp.broadcast_to(r, (SUB, d)) for r in vals], axis=0)

    def chunk_body(c, carry):
        r0 = pl.multiple_of(c * CHUNK, CHUNK)
        rows = pl.ds(r0, CHUNK)
        g_hi, g_lo = _split_bf16(g_ref[rows, :])
        big_g = _dot(tri, g_hi) + _dot(tri, g_lo)
        q = q_ref[rows, :]
        k = k_ref[rows, :]

        ends = [big_g[SUB * b + SUB - 1:SUB * b + SUB, :] for b in range(nsub)]
        prevs = [jnp.zeros((1, d), f32)] + ends[:-1]
        ref_prev = block_rows(prevs)
        ref_end = block_rows(ends)
        qw = q * jnp.exp(big_g - ref_prev)
        k_end = k * jnp.exp(ref_end - big_g)
        qe = (qw * block_rows([jnp.exp(p) for p in prevs])).astype(bf16)
        k_dec = (k_end * block_rows([jnp.exp(ends[-1] - e) for e in ends])).astype(bf16)
        half = 0.5 * (ref_prev - ref_end)
        q_mid = (q * jnp.exp(jnp.clip(big_g - ref_prev + half, -EXP_CLAMP, EXP_CLAMP))).astype(bf16)
        k_mid = (k * jnp.exp(jnp.clip(ref_end - big_g + half, -EXP_CLAMP, EXP_CLAMP))).astype(bf16)
        q_parts, k_parts = [], []
        for jb in range(nsub - 1):
            qp = [zeros_blk] * (jb + 1)
            for b in range(jb + 1, nsub):
                piece = qw[SUB * b:SUB * (b + 1), :]
                if b > jb + 1:
                    piece = piece * jnp.exp(prevs[b] - ends[jb])
                qp.append(piece)
            q_parts.append(jnp.concatenate(qp, axis=0).astype(bf16))
            kp = [zeros_blk] * nsub
            kp[jb] = k_end[SUB * jb:SUB * (jb + 1), :]
            k_parts.append(jnp.concatenate(kp, axis=0).astype(bf16))
        decay = jnp.exp(ends[-1])

        for hd in range(heads):
            sl = slice(hd * HEAD, (hd + 1) * HEAD)
            v = v_ref[rows, sl]
            st = st_ref[hd]
            s_off = _dot_nt(jnp.concatenate([p[:, sl] for p in q_parts], axis=1),
                            jnp.concatenate([p[:, sl] for p in k_parts], axis=1))
            s_diag = _dot_nt(q_mid[:, sl], k_mid[:, sl])
            scores = (s_off + jnp.where(diag_mask, s_diag, 0.0)).astype(bf16)
            o = _dot_nt(qe[:, sl], st.astype(bf16)) + _dot(scores, v)
            st_ref[hd] = decay[:, sl] * st + _dot_tn(v, k_dec[:, sl])
            o = o * lax.rsqrt(jnp.mean(o * o, axis=-1, keepdims=True) + EPS) * ng
            oh_ref[rows, sl] = (o * og_ref[rows, sl]).astype(bf16)
        return carry

    lax.fori_loop(0, tm // CHUNK, chunk_body, 0, unroll=2)
    out = _dot(oh_ref[...], wout_ref[...])
    o_ref[...] = x + _rms(out, gpost_ref[...])


def _hgrn_mixer(h, gpre, gpost, w_in, lower_bounds, norm_g, w_out, moe_w_gu, moe_w_down,
                *, tm, layer):
    b, l, d = h.shape
    depth = lower_bounds.shape[0]
    n_tiles = l // tm
    steps = b * n_tiles
    wgu = moe_w_gu.reshape(-1, moe_w_gu.shape[-1])
    wdn = moe_w_down.reshape(-1, moe_w_down.shape[-1])
    gu_rows, dn_rows = wgu.shape[0] // steps, wdn.shape[0] // steps
    assert gu_rows * steps == wgu.shape[0] and dn_rows * steps == wdn.shape[0]
    slice_map = lambda bi, i: (bi * n_tiles + i, 0)
    kern = functools.partial(_hgrn_kernel, tm=tm, d=d, layer=layer)
    out, wgu_bf, wdn_bf = pl.pallas_call(
        kern,
        grid=(b, n_tiles),
        in_specs=[
            pl.BlockSpec((None, tm, d), lambda bi, i: (bi, i, 0)),
            _const_spec((1, d)), _const_spec((1, d)),
            _const_spec((d, 4 * d)), _const_spec((depth, d)), _const_spec((1, HEAD)),
            _const_spec((d, d)),
            pl.BlockSpec((gu_rows, wgu.shape[1]), slice_map),
            pl.BlockSpec((dn_rows, wdn.shape[1]), slice_map),
        ],
        out_specs=[
            pl.BlockSpec((None, tm, d), lambda bi, i: (bi, i, 0)),
            pl.BlockSpec((gu_rows, wgu.shape[1]), slice_map),
            pl.BlockSpec((dn_rows, wdn.shape[1]), slice_map),
        ],
        out_shape=[
            jax.ShapeDtypeStruct((b, l, d), f32),
            jax.ShapeDtypeStruct(wgu.shape, bf16),
            jax.ShapeDtypeStruct(wdn.shape, bf16),
        ],
        scratch_shapes=[pltpu.VMEM((tm, d), f32), pltpu.VMEM((tm, d), f32),
                        pltpu.VMEM((tm, d), bf16), pltpu.VMEM((tm, d), f32),
                        pltpu.VMEM((tm, d), f32), pltpu.VMEM((tm, d), bf16),
                        pltpu.VMEM((d // HEAD, HEAD, HEAD), f32)],
        compiler_params=pltpu.CompilerParams(
            dimension_semantics=("arbitrary", "arbitrary"), vmem_limit_bytes=VMEM_LIMIT),
        name="hgrn_mixer",
    )(h, gpre.reshape(1, d), gpost.reshape(1, d), w_in, lower_bounds, norm_g.reshape(1, HEAD), w_out,
      wgu, wdn)
    return out, wgu_bf.reshape(moe_w_gu.shape), wdn_bf.reshape(moe_w_down.shape)


def _router_kernel(h_ref, gpre_ref, wr_ref, up_ref, pos_ref, gate_ref, cnt_ref, carry_ref,
                   *, tm, d, n_exp):
    i = pl.program_id(0)

    @pl.when(i == 0)
    def _():
        carry_ref[...] = jnp.zeros(carry_ref.shape, f32)

    u = _rms(h_ref[...], gpre_ref[...])
    up_ref[...] = u.reshape(up_ref.shape)

    u_hi, u_lo = _split_bf16(u)
    w_hi, w_lo = _split_bf16(wr_ref[...])
    logits = _dot(u_hi, w_hi) + (_dot(u_hi, w_lo) + _dot(u_lo, w_hi))
    lane = lax.broadcasted_iota(jnp.int32, (tm, LANES), 1)
    neg = jnp.float32(-jnp.inf)
    lg = jnp.where(lane < n_exp, logits, neg)
    v1 = jnp.max(lg, axis=-1, keepdims=True)
    i1 = jnp.min(jnp.where(lg == v1, lane, LANES), axis=-1, keepdims=True)
    m1 = lane == i1
    lg2 = jnp.where(m1, neg, lg)
    v2 = jnp.max(lg2, axis=-1, keepdims=True)
    i2 = jnp.min(jnp.where(lg2 == v2, lane, LANES), axis=-1, keepdims=True)
    m2 = lane == i2
    dd = jnp.exp(v2 - v1)
    w1 = 1.0 / (1.0 + dd)
    w2 = dd / (1.0 + dd)

    sel = (m1 | m2).astype(f32)
    rr = lax.broadcasted_iota(jnp.int32, (tm, tm), 0)
    cc = lax.broadcasted_iota(jnp.int32, (tm, tm), 1)
    cum = _dot((rr >= cc).astype(bf16), sel.astype(bf16))
    rank = cum - sel + carry_ref[...]
    r1 = jnp.sum(jnp.where(m1, rank, 0.0), axis=-1, keepdims=True).astype(jnp.int32)
    r2 = jnp.sum(jnp.where(m2, rank, 0.0), axis=-1, keepdims=True).astype(jnp.int32)
    pos_ref[...] = jnp.where(lane == 0, r1, jnp.where(lane == 1, r2, jnp.where(
        lane == 2, i1, jnp.where(lane == 3, i2, 0))))
    gate_ref[...] = jnp.where(lane == 0, w1, jnp.where(lane == 1, w2, 0.0))
    carry_ref[...] = carry_ref[...] + cum[tm - 1:tm, :]
    cnt_ref[...] = carry_ref[...].astype(jnp.int32)


def _router(h, gpre, w_router, *, tm):
    t, d = h.shape
    n_exp = w_router.shape[1]
    wr = jnp.zeros((d, LANES), f32).at[:, :n_exp].set(w_router)
    kern = functools.partial(_router_kernel, tm=tm, d=d, n_exp=n_exp)
    return pl.pallas_call(
        kern,
        grid=(t // tm,),
        in_specs=[
            pl.BlockSpec((tm, d), lambda i: (i, 0)),
            _const_spec((1, d)), _const_spec((d, LANES)),
        ],
        out_specs=[
            pl.BlockSpec((tm,) + _row_slabs(d), lambda i: (i, 0, 0)),
            pl.BlockSpec((tm, LANES), lambda i: (i, 0)),
            pl.BlockSpec((tm, LANES), lambda i: (i, 0)),
            pl.BlockSpec((1, LANES), lambda i: (0, 0)),
        ],
        out_shape=[
            jax.ShapeDtypeStruct((t,) + _row_slabs(d), f32),
            jax.ShapeDtypeStruct((t, LANES), jnp.int32),
            jax.ShapeDtypeStruct((t, LANES), f32),
            jax.ShapeDtypeStruct((1, LANES), jnp.int32),
        ],
        scratch_shapes=[pltpu.VMEM((1, LANES), f32)],
        compiler_params=pltpu.CompilerParams(
            dimension_semantics=("arbitrary",), vmem_limit_bytes=VMEM_LIMIT),
        name="moe_router",
    )(h, gpre.reshape(1, d), wr)


def _dispatch_kernel(cnt_ref, start_ref, nv_ref, pos_ref, up_ref, xs_ref, tile_ref, zero_ref,
                     fsem, sem, zsem, *, tm, te, n_exp, nt):
    i = pl.program_id(0)
    last = pl.num_programs(0) - 1
    slot = i % RING

    def fetch(step, sl):
        src = up_ref.at[pl.ds(pl.multiple_of(step * tm, tm), tm)]
        return pltpu.make_async_copy(src, tile_ref.at[sl], fsem.at[sl])

    def drain_step(sl):
        for _ in range(TOP_K):
            pltpu.make_async_copy(tile_ref.at[sl], xs_ref.at[pl.ds(0, tm)], sem.at[sl]).wait()

    @pl.when(i == 0)
    def _():
        fetch(0, 0).start()

    @pl.when(i < last)
    def _():
        fetch(i + 1, (i + 1) % RING).start()

    fetch(i, slot).wait()

    def issue(r, carry):
        for kk in range(TOP_K):
            pltpu.make_async_copy(tile_ref.at[slot, pl.ds(r, 1)],
                                  xs_ref.at[pl.ds(pos_ref[TOP_K * r + kk], 1)],
                                  sem.at[slot]).start(priority=kk)
        return carry

    lax.fori_loop(0, tm, issue, 0, unroll=8)

    @pl.when(i > 0)
    def _():
        drain_step((i + RING - 1) % RING)

    @pl.when(i == last)
    def _():
        drain_step(slot)
        zero_ref[...] = jnp.zeros(zero_ref.shape, zero_ref.dtype)

        def zero_row(dst):
            return pltpu.make_async_copy(zero_ref.at[pl.ds(0, 1)], xs_ref.at[pl.ds(dst, 1)], zsem)

        def zero_tile(ti):
            dst = pl.multiple_of(ti * te, te)
            return pltpu.make_async_copy(zero_ref, xs_ref.at[pl.ds(dst, te)], zsem)

        for ex in range(n_exp):
            lo = start_ref[ex] + cnt_ref[ex]
            hi = start_ref[ex] + ((cnt_ref[ex] + te - 1) // te) * te
            lax.fori_loop(lo, hi, lambda r, c: (zero_row(r).start(), c)[1], 0)
            lax.fori_loop(lo, hi, lambda r, c: (zero_row(r).wait(), c)[1], 0)
        lax.fori_loop(nv_ref[0], nt, lambda ti, c: (zero_tile(ti).start(), c)[1], 0)
        lax.fori_loop(nv_ref[0], nt, lambda ti, c: (zero_tile(ti).wait(), c)[1], 0)


def _dispatch(counts, starts, nvalid, pos_flat, upk, *, tm, te, n_exp, nt):
    t, row = upk.shape[0], upk.shape[1:]
    kern = functools.partial(_dispatch_kernel, tm=tm, te=te, n_exp=n_exp, nt=nt)
    grid_spec = pltpu.PrefetchScalarGridSpec(
        num_scalar_prefetch=3,
        grid=(t // tm,),
        in_specs=[
            pl.BlockSpec((TOP_K * tm,), lambda i, *_: (i,), memory_space=pltpu.SMEM),
            pl.BlockSpec(memory_space=pl.ANY),
        ],
        out_specs=pl.BlockSpec(memory_space=pl.ANY),
        scratch_shapes=[pltpu.VMEM((RING, tm) + row, upk.dtype), pltpu.VMEM((te,) + row, upk.dtype),
                        pltpu.SemaphoreType.DMA((RING,)), pltpu.SemaphoreType.DMA((RING,)),
                        pltpu.SemaphoreType.DMA],
    )
    return pl.pallas_call(
        kern,
        grid_spec=grid_spec,
        out_shape=jax.ShapeDtypeStruct((nt * te,) + row, upk.dtype),
        compiler_params=pltpu.CompilerParams(
            dimension_semantics=("arbitrary",), vmem_limit_bytes=VMEM_LIMIT),
        name="moe_dispatch",
    )(counts, starts, nvalid, pos_flat, upk)


def _expert_kernel(te_ref, nv_ref, xs_ref, wg_ref, wu_ref, wd_ref, y_ref, x_ref, a_ref, acc_ref,
                   *, fc, sc, nf):
    i = pl.program_id(0)
    j = pl.program_id(1)

    @pl.when((i >= nv_ref[0]) & (j == 0))
    def _():
        y_ref[...] = jnp.zeros(y_ref.shape, f32)

    @pl.when((i < nv_ref[0]) & (j == 0))
    def _():
        x_ref[...] = xs_ref[...].reshape(x_ref.shape).astype(bf16)

    @pl.when(i < nv_ref[0])
    def _():
        x = x_ref[...]
        for c in range(fc // sc):
            gate = _dot(x, wg_ref[:, c * sc:(c + 1) * sc])
            up = _dot(x, wu_ref[:, c * sc:(c + 1) * sc])
            a_ref[:, c * sc:(c + 1) * sc] = (_silu(gate) * up).astype(bf16)
        part = _dot(a_ref[...], wd_ref[...])

        @pl.when(j == 0)
        def _():
            acc_ref[...] = part

        @pl.when(j > 0)
        def _():
            acc_ref[...] = acc_ref[...] + part

        @pl.when(j == nf - 1)
        def _():
            y_ref[...] = acc_ref[...].reshape(y_ref.shape)


def _experts(tile_e, nvalid, xs, w_gu, w_down, *, te, fc, sc):
    rows, row = xs.shape[0], xs.shape[1:]
    n_exp, d, f2 = w_gu.shape
    f = f2 // 2
    nf = f // fc
    nt = tile_e.shape[0]

    def jj(i, j, nv):
        return jnp.where(i < nv[0], j, nf - 1)

    grid_spec = pltpu.PrefetchScalarGridSpec(
        num_scalar_prefetch=2,
        grid=(nt, nf),
        in_specs=[
            pl.BlockSpec((te,) + row, lambda i, j, e, nv: (i, 0, 0)),
            pl.BlockSpec((None, d, fc), lambda i, j, e, nv: (e[i], 0, jj(i, j, nv))),
            pl.BlockSpec((None, d, fc), lambda i, j, e, nv: (e[i], 0, nf + jj(i, j, nv))),
            pl.BlockSpec((None, fc, d), lambda i, j, e, nv: (e[i], jj(i, j, nv), 0)),
        ],
        out_specs=pl.BlockSpec((te,) + row, lambda i, j, e, nv: (i, 0, 0)),
        scratch_shapes=[pltpu.VMEM((te, d), bf16), pltpu.VMEM((te, fc), bf16),
                        pltpu.VMEM((te, d), f32)],
    )
    kern = functools.partial(_expert_kernel, fc=fc, sc=sc, nf=nf)
    return pl.pallas_call(
        kern,
        grid_spec=grid_spec,
        out_shape=jax.ShapeDtypeStruct((rows,) + row, f32),
        compiler_params=pltpu.CompilerParams(
            dimension_semantics=("arbitrary", "arbitrary"), vmem_limit_bytes=VMEM_LIMIT),
        name="moe_experts",
    )(tile_e, nvalid, xs, w_gu, w_gu, w_down)


def _combine_kernel(pos_ref, posn_ref, y_ref, gate_ref, h_ref, gpost_ref, o_ref, buf_ref, sem,
                    *, tm):
    i = pl.program_id(0)
    slot = i % 2

    def issue(p_ref, sl):
        def body(r, carry):
            for kk in range(TOP_K):
                pltpu.make_async_copy(y_ref.at[pl.ds(p_ref[TOP_K * r + kk], 1)],
                                      buf_ref.at[sl, kk, pl.ds(r, 1)], sem.at[sl]).start()
            return carry
        lax.fori_loop(0, tm, body, 0, unroll=8)

    @pl.when(i == 0)
    def _():
        issue(pos_ref, 0)

    @pl.when(i + 1 < pl.num_programs(0))
    def _():
        issue(posn_ref, 1 - slot)

    for kk in range(TOP_K):
        pltpu.make_async_copy(y_ref.at[pl.ds(0, tm)], buf_ref.at[slot, kk], sem.at[slot]).wait()
    gates = gate_ref[...]
    shape = h_ref.shape
    mix = (gates[:, 0:1] * buf_ref[slot, 0].reshape(shape)
           + gates[:, 1:2] * buf_ref[slot, 1].reshape(shape))
    o_ref[...] = h_ref[...] + _rms(mix, gpost_ref[...])


def _combine(pos_flat, y, gates, h, gpost, *, tm):
    t, d = h.shape
    kern = functools.partial(_combine_kernel, tm=tm)
    n_steps = t // tm
    return pl.pallas_call(
        kern,
        grid=(n_steps,),
        in_specs=[
            pl.BlockSpec((TOP_K * tm,), lambda i: (i,), memory_space=pltpu.SMEM),
            pl.BlockSpec((TOP_K * tm,), lambda i: (jnp.minimum(i + 1, n_steps - 1),),
                         memory_space=pltpu.SMEM),
            pl.BlockSpec(memory_space=pl.ANY),
            pl.BlockSpec((tm, LANES), lambda i: (i, 0)),
            pl.BlockSpec((tm, d), lambda i: (i, 0)),
            _const_spec((1, d)),
        ],
        out_specs=pl.BlockSpec((tm, d), lambda i: (i, 0)),
        out_shape=jax.ShapeDtypeStruct((t, d), f32),
        scratch_shapes=[pltpu.VMEM((2, TOP_K, tm) + _row_slabs(d), f32),
                        pltpu.SemaphoreType.DMA((2,))],
        compiler_params=pltpu.CompilerParams(
            dimension_semantics=("arbitrary",), vmem_limit_bytes=VMEM_LIMIT),
        name="moe_combine",
    )(pos_flat, pos_flat, y, gates, h, gpost.reshape(1, d))


def _moe(h, gpre, gpost, w_router, w_gu, w_down, *, tm_route, tm_move, te, fc, sc):
    t, d = h.shape
    n_exp = w_router.shape[1]
    upk, pos, gates, counts = _router(h, gpre, w_router, tm=tm_route)
    counts = counts[0, :n_exp]

    nt = (TOP_K * t) // te + n_exp
    tiles_per = (counts + te - 1) // te
    tile_start = jnp.cumsum(tiles_per) - tiles_per
    nvalid = jnp.sum(tiles_per).reshape(1).astype(jnp.int32)
    idx = jnp.minimum(jnp.arange(nt, dtype=jnp.int32), jnp.maximum(nvalid - 1, 0))
    tile_e = (jnp.sum(idx[:, None] >= tile_start[None, :], axis=1) - 1).astype(jnp.int32)
    row_start = (tile_start * te).astype(jnp.int32)
    pos_flat = (pos[:, :TOP_K] + row_start[pos[:, TOP_K:2 * TOP_K]]).reshape(-1)

    xs = _dispatch(counts, row_start, nvalid, pos_flat, upk, tm=tm_move, te=te, n_exp=n_exp, nt=nt)
    y = _experts(tile_e, nvalid, xs, w_gu, w_down, te=te, fc=fc, sc=sc)
    return _combine(pos_flat, y, gates, h, gpost, tm=tm_move)


def _pick(n, pref):
    return pref if n % pref == 0 else n


def kernel(x, norm_g, conv_w_in, conv_b_in, conv_dw_w, conv_dw_b, conv_ln_g, conv_ln_b, conv_w_out, conv_b_out, hgrn_w_in, hgrn_lower_bounds, hgrn_norm_g, hgrn_w_out, ffn_w_gu, ffn_w_down, moe_router, moe_w_gu, moe_w_down):
    b, l, d = x.shape
    depth = norm_g.shape[0]
    t = b * l
    tm_seq = _pick(l, 512)
    tm_tok = _pick(t, 512)
    h = x
    for i in range(depth):
        j = i // 2
        if i % 2 == 0:
            h = _conv_mixer(h, norm_g[i, 0], norm_g[i, 1], conv_w_in[j].astype(bf16), conv_b_in[j],
                            conv_dw_w[j], conv_dw_b[j], conv_ln_g[j], conv_ln_b[j],
                            conv_w_out[j].astype(bf16), conv_b_out[j], tm=tm_seq)
            f = ffn_w_down.shape[1]
            h = _ffn(h.reshape(t, d), norm_g[i, 2], norm_g[i, 3], ffn_w_gu[j].astype(bf16),
                     ffn_w_down[j].astype(bf16), tm=tm_tok, fc=_pick(f, 256)).reshape(b, l, d)
        else:
            h, w_gu_bf, w_down_bf = _hgrn_mixer(
                h, norm_g[i, 0], norm_g[i, 1], hgrn_w_in[j].astype(bf16), hgrn_lower_bounds,
                hgrn_norm_g[j], hgrn_w_out[j].astype(bf16), moe_w_gu[j], moe_w_down[j],
                tm=tm_seq, layer=i)
            f = moe_w_down.shape[2]
            fc = f // 2 if (f // 2) % LANES == 0 else f
            h = _moe(h.reshape(t, d), norm_g[i, 2], norm_g[i, 3], moe_router[j],
                     w_gu_bf, w_down_bf,
                     tm_route=tm_tok, tm_move=_pick(t, 256), te=_pick(t, 512), fc=fc,
                     sc=_pick(fc, 256)).reshape(b, l, d)
    return h
```

```python
import functools

import jax
import jax.numpy as jnp
from jax import lax
from jax.experimental import pallas as pl
from jax.experimental.pallas import tpu as pltpu

EPS = 1e-6
CHUNK = 64
SUB = 16
HEAD = 128
CONV_WIDTH = 31
HALO = 32
CONV_ROWS = 64
LN_ROWS = 32
ROW_PITCH = 2
LANES = 128
SUBLANES = 8
TOP_K = 2
RING = 3
EXP_CLAMP = 80.0
VMEM_LIMIT = 56 * 1024 * 1024

f32 = jnp.float32
bf16 = jnp.bfloat16


def _dot(a, b):
    return jnp.dot(a, b, preferred_element_type=f32)


def _dot_nt(a, b):
    return lax.dot_general(a, b, (((1,), (1,)), ((), ())), preferred_element_type=f32)


def _dot_tn(a, b):
    return lax.dot_general(a, b, (((0,), (0,)), ((), ())), preferred_element_type=f32)


def _rms(x, g):
    return x * lax.rsqrt(jnp.mean(x * x, axis=-1, keepdims=True) + EPS) * g


def _silu(x):
    return x * jax.nn.sigmoid(x)


def _to_slabs(ref, x):
    n, d = x.shape
    s_n = d // LANES
    for s in range(s_n):
        ref[pl.ds(s, n, stride=s_n), :] = x[:, s * LANES:(s + 1) * LANES]


def _from_slabs(ref, n, d):
    s_n = d // LANES
    return jnp.concatenate([ref[pl.ds(s, n, stride=s_n), :] for s in range(s_n)], axis=1)


def _split_bf16(x):
    hi = x.astype(bf16)
    return hi, (x - hi.astype(f32)).astype(bf16)


def _const_spec(shape):
    nd = len(shape)
    return pl.BlockSpec(shape, lambda *_: (0,) * nd, pipeline_mode=pl.Buffered(1))


def _conv_mixer_kernel(x_ref, gpre_ref, gpost_ref, win_ref, bin_ref, dww_ref, dwb_ref,
                       lng_ref, lnb_ref, wout_ref, bout_ref, o_ref, gl_ref, y_ref, z_ref,
                       *, tm, d):
    i = pl.program_id(1)
    nslab = d // LANES

    def rows(start, n):
        return pl.ds(ROW_PITCH * start, n, stride=ROW_PITCH)

    @pl.when(i == 0)
    def _():
        for c in range(nslab):
            gl_ref[c, rows(0, HALO), :] = jnp.zeros((HALO, LANES), f32)

    @pl.when(i > 0)
    def _():
        for c in range(nslab):
            gl_ref[c, rows(0, HALO), :] = gl_ref[c, rows(tm, HALO), :]

    x = x_ref[...]
    u = _rms(x, gpre_ref[...]).astype(bf16)
    a = _dot(u, win_ref[:, :d]) + bin_ref[:, :d]
    gate = _dot(u, win_ref[:, d:]) + bin_ref[:, d:]
    glu = a * jax.nn.sigmoid(gate)
    for c in range(nslab):
        gl_ref[c, rows(HALO, tm), :] = glu[:, c * LANES:(c + 1) * LANES]

    off = HALO - (CONV_WIDTH - 1)
    for c in range(nslab):
        cols = slice(c * LANES, (c + 1) * LANES)
        for rb in range(tm // CONV_ROWS):
            base = rb * CONV_ROWS
            acc = jnp.broadcast_to(dwb_ref[:, cols], (CONV_ROWS, LANES))
            for j in range(CONV_WIDTH):
                acc = acc + dww_ref[j:j + 1, cols] * gl_ref[c, rows(base + off + j, CONV_ROWS), :]
            y_ref[base:base + CONV_ROWS, cols] = acc

    for rb in range(tm // LN_ROWS):
        base = rb * LN_ROWS
        acc = y_ref[base:base + LN_ROWS, :]
        mu = jnp.mean(acc, axis=-1, keepdims=True)
        xc = acc - mu
        y = xc * lax.rsqrt(jnp.mean(xc * xc, axis=-1, keepdims=True) + EPS)
        y = y * lng_ref[...] + lnb_ref[...]
        z_ref[base:base + LN_ROWS, :] = _silu(y).astype(bf16)

    out = _dot(z_ref[...], wout_ref[...]) + bout_ref[...]
    o_ref[...] = x + _rms(out, gpost_ref[...])


def _conv_mixer(x, gpre, gpost, w_in, b_in, dw_w, dw_b, ln_g, ln_b, w_out, b_out, *, tm):
    b, l, d = x.shape
    row = lambda v: v.reshape(1, -1)
    kern = functools.partial(_conv_mixer_kernel, tm=tm, d=d)
    return pl.pallas_call(
        kern,
        grid=(b, l // tm),
        in_specs=[
            pl.BlockSpec((None, tm, d), lambda bi, i: (bi, i, 0)),
            _const_spec((1, d)), _const_spec((1, d)),
            _const_spec((d, 2 * d)), _const_spec((1, 2 * d)),
            _const_spec((CONV_WIDTH, d)), _const_spec((1, d)),
            _const_spec((1, d)), _const_spec((1, d)),
            _const_spec((d, d)), _const_spec((1, d)),
        ],
        out_specs=pl.BlockSpec((None, tm, d), lambda bi, i: (bi, i, 0)),
        out_shape=jax.ShapeDtypeStruct((b, l, d), f32),
        scratch_shapes=[pltpu.VMEM((d // LANES, ROW_PITCH * (tm + HALO), LANES), f32),
                        pltpu.VMEM((tm, d), f32), pltpu.VMEM((tm, d), bf16)],
        compiler_params=pltpu.CompilerParams(
            dimension_semantics=("arbitrary", "arbitrary"), vmem_limit_bytes=VMEM_LIMIT),
        name="conv_mixer",
    )(x, row(gpre), row(gpost), w_in, row(b_in), dw_w, row(dw_b), row(ln_g), row(ln_b),
      w_out, row(b_out))


def _ffn_kernel(h_ref, gpre_ref, gpost_ref, wgu_ref, wd_ref, o_ref, a_ref, *, f, fc):
    x = h_ref[...]
    u = _rms(x, gpre_ref[...]).astype(bf16)
    for c in range(f // fc):
        gate = _dot(u, wgu_ref[:, c * fc:(c + 1) * fc])
        up = _dot(u, wgu_ref[:, f + c * fc:f + (c + 1) * fc])
        a_ref[:, c * fc:(c + 1) * fc] = (_silu(gate) * up).astype(bf16)
    out = _dot(a_ref[...], wd_ref[...])
    o_ref[...] = x + _rms(out, gpost_ref[...])


def _ffn(h, gpre, gpost, w_gu, w_down, *, tm, fc):
    t, d = h.shape
    f = w_down.shape[0]
    kern = functools.partial(_ffn_kernel, f=f, fc=fc)
    return pl.pallas_call(
        kern,
        grid=(t // tm,),
        in_specs=[
            pl.BlockSpec((tm, d), lambda i: (i, 0)),
            _const_spec((1, d)), _const_spec((1, d)),
            _const_spec((d, 2 * f)), _const_spec((f, d)),
        ],
        out_specs=pl.BlockSpec((tm, d), lambda i: (i, 0)),
        out_shape=jax.ShapeDtypeStruct((t, d), f32),
        scratch_shapes=[pltpu.VMEM((tm, f), bf16)],
        compiler_params=pltpu.CompilerParams(
            dimension_semantics=("arbitrary",), vmem_limit_bytes=VMEM_LIMIT),
        name="dense_ffn",
    )(h, gpre.reshape(1, d), gpost.reshape(1, d), w_gu, w_down)


def _hgrn_kernel(h_ref, gpre_ref, gpost_ref, win_ref, lbp_ref, ng_ref, wout_ref, wgu_ref, wdn_ref,
                 o_ref, wgu_bf_ref, wdn_bf_ref,
                 q_ref, k_ref, v_ref, g_ref, og_ref, oh_ref, st_ref, *, tm, d, layer):
    i = pl.program_id(1)
    heads = d // HEAD
    nsub = CHUNK // SUB

    wgu_bf_ref[...] = wgu_ref[...].astype(bf16)
    wdn_bf_ref[...] = wdn_ref[...].astype(bf16)

    @pl.when(i == 0)
    def _():
        st_ref[...] = jnp.zeros(st_ref.shape, f32)

    x = h_ref[...]
    u = _rms(x, gpre_ref[...]).astype(bf16)

    lbp = lbp_ref[...]
    e = jnp.exp(lbp - jnp.max(lbp, axis=0, keepdims=True))
    sm = e / jnp.sum(e, axis=0, keepdims=True)
    lb = jnp.sum(sm[1:layer + 1, :], axis=0, keepdims=True)

    q_ref[...] = _silu(_dot(u, win_ref[:, 0:d]))
    forget = lb + (1.0 - lb) * jax.nn.sigmoid(_dot(u, win_ref[:, d:2 * d]))
    k_ref[...] = 1.0 - forget
    g_ref[...] = jnp.log(forget)
    v_ref[...] = _dot(u, win_ref[:, 2 * d:3 * d]).astype(bf16)
    og_ref[...] = _silu(_dot(u, win_ref[:, 3 * d:4 * d]))

    rr = lax.broadcasted_iota(jnp.int32, (CHUNK, CHUNK), 0)
    cc = lax.broadcasted_iota(jnp.int32, (CHUNK, CHUNK), 1)
    tri = (rr >= cc).astype(bf16)
    diag_mask = (rr >= cc) & ((rr // SUB) == (cc // SUB))
    ng = ng_ref[...]
    zeros_blk = jnp.zeros((SUB, d), f32)

    def block_rows(vals):
        return jnp.concatenate([jnp.broadcast_to(r, (SUB, d)) for r in vals], axis=0)

    def chunk_body(c, carry):
        r0 = pl.multiple_of(c * CHUNK, CHUNK)
        rows = pl.ds(r0, CHUNK)
        g_hi, g_lo = _split_bf16(g_ref[rows, :])
        big_g = _dot(tri, g_hi) + _dot(tri, g_lo)
        q = q_ref[rows, :]
        k = k_ref[rows, :]

        ends = [big_g[SUB * b + SUB - 1:SUB * b + SUB, :] for b in range(nsub)]
        prevs = [jnp.zeros((1, d), f32)] + ends[:-1]
        ref_prev = block_rows(prevs)
        ref_end = block_rows(ends)
        qw = q * jnp.exp(big_g - ref_prev)
        k_end = k * jnp.exp(ref_end - big_g)
        qe = (qw * block_rows([jnp.exp(p) for p in prevs])).astype(bf16)
        k_dec = (k_end * block_rows([jnp.exp(ends[-1] - e) for e in ends])).astype(bf16)
        half = 0.5 * (ref_prev - ref_end)
        q_mid = (q * jnp.exp(jnp.clip(big_g - ref_prev + half, -EXP_CLAMP, EXP_CLAMP))).astype(bf16)
        k_mid = (k * jnp.exp(jnp.clip(ref_end - big_g + half, -EXP_CLAMP, EXP_CLAMP))).astype(bf16)
        q_parts, k_parts = [], []
        for jb in range(nsub - 1):
            qp = [zeros_blk] * (jb + 1)
            for b in range(jb + 1, nsub):
                piece = qw[SUB * b:SUB * (b + 1), :]
                if b > jb + 1:
                    piece = piece * jnp.exp(prevs[b] - ends[jb])
                qp.append(piece)
            q_parts.append(jnp.concatenate(qp, axis=0).astype(bf16))
            kp = [zeros_blk] * nsub
            kp[jb] = k_end[SUB * jb:SUB * (jb + 1), :]
            k_parts.append(jnp.concatenate(kp, axis=0).astype(bf16))
        decay = jnp.exp(ends[-1])

        for hd in range(heads):
            sl = slice(hd * HEAD, (hd + 1) * HEAD)
            v = v_ref[rows, sl]
            st = st_ref[hd]
            s_off = _dot_nt(jnp.concatenate([p[:, sl] for p in q_parts], axis=1),
                            jnp.concatenate([p[:, sl] for p in k_parts], axis=1))
            s_diag = _dot_nt(q_mid[:, sl], k_mid[:, sl])
            scores = (s_off + jnp.where(diag_mask, s_diag, 0.0)).astype(bf16)
            o = _dot_nt(qe[:, sl], st.astype(bf16)) + _dot(scores, v)
            st_ref[hd] = decay[:, sl] * st + _dot_tn(v, k_dec[:, sl])
            o = o * lax.rsqrt(jnp.mean(o * o, axis=-1, keepdims=True) + EPS) * ng
            oh_ref[rows, sl] = (o * og_ref[rows, sl]).astype(bf16)
        return carry

    lax.fori_loop(0, tm // CHUNK, chunk_body, 0, unroll=2)
    out = _dot(oh_ref[...], wout_ref[...])
    o_ref[...] = x + _rms(out, gpost_ref[...])


def _hgrn_mixer(h, gpre, gpost, w_in, lower_bounds, norm_g, w_out, moe_w_gu, moe_w_down,
                *, tm, layer):
    b, l, d = h.shape
    depth = lower_bounds.shape[0]
    n_tiles = l // tm
    steps = b * n_tiles
    wgu = moe_w_gu.reshape(-1, moe_w_gu.shape[-1])
    wdn = moe_w_down.reshape(-1, moe_w_down.shape[-1])
    gu_rows, dn_rows = wgu.shape[0] // steps, wdn.shape[0] // steps
    assert gu_rows * steps == wgu.shape[0] and dn_rows * steps == wdn.shape[0]
    slice_map = lambda bi, i: (bi * n_tiles + i, 0)
    kern = functools.partial(_hgrn_kernel, tm=tm, d=d, layer=layer)
    out, wgu_bf, wdn_bf = pl.pallas_call(
        kern,
        grid=(b, n_tiles),
        in_specs=[
            pl.BlockSpec((None, tm, d), lambda bi, i: (bi, i, 0)),
            _const_spec((1, d)), _const_spec((1, d)),
            _const_spec((d, 4 * d)), _const_spec((depth, d)), _const_spec((1, HEAD)),
            _const_spec((d, d)),
            pl.BlockSpec((gu_rows, wgu.shape[1]), slice_map),
            pl.BlockSpec((dn_rows, wdn.shape[1]), slice_map),
        ],
        out_specs=[
            pl.BlockSpec((None, tm, d), lambda bi, i: (bi, i, 0)),
            pl.BlockSpec((gu_rows, wgu.shape[1]), slice_map),
            pl.BlockSpec((dn_rows, wdn.shape[1]), slice_map),
        ],
        out_shape=[
            jax.ShapeDtypeStruct((b, l, d), f32),
            jax.ShapeDtypeStruct(wgu.shape, bf16),
            jax.ShapeDtypeStruct(wdn.shape, bf16),
        ],
        scratch_shapes=[pltpu.VMEM((tm, d), f32), pltpu.VMEM((tm, d), f32),
                        pltpu.VMEM((tm, d), bf16), pltpu.VMEM((tm, d), f32),
                        pltpu.VMEM((tm, d), f32), pltpu.VMEM((tm, d), bf16),
                        pltpu.VMEM((d // HEAD, HEAD, HEAD), f32)],
        compiler_params=pltpu.CompilerParams(
            dimension_semantics=("arbitrary", "arbitrary"), vmem_limit_bytes=VMEM_LIMIT),
        name="hgrn_mixer",
    )(h, gpre.reshape(1, d), gpost.reshape(1, d), w_in, lower_bounds, norm_g.reshape(1, HEAD), w_out,
      wgu, wdn)
    return out, wgu_bf.reshape(moe_w_gu.shape), wdn_bf.reshape(moe_w_down.shape)


def _router_kernel(h_ref, gpre_ref, wr_ref, up_ref, pos_ref, gate_ref, cnt_ref, carry_ref,
                   *, tm, d, n_exp):
    i = pl.program_id(0)

    @pl.when(i == 0)
    def _():
        carry_ref[...] = jnp.zeros(carry_ref.shape, f32)

    u = _rms(h_ref[...], gpre_ref[...])
    _to_slabs(up_ref, u)

    u_hi, u_lo = _split_bf16(u)
    w_hi, w_lo = _split_bf16(wr_ref[...])
    logits = _dot(u_hi, w_hi) + (_dot(u_hi, w_lo) + _dot(u_lo, w_hi))
    lane = lax.broadcasted_iota(jnp.int32, (tm, LANES), 1)
    neg = jnp.float32(-jnp.inf)
    lg = jnp.where(lane < n_exp, logits, neg)
    v1 = jnp.max(lg, axis=-1, keepdims=True)
    i1 = jnp.min(jnp.where(lg == v1, lane, LANES), axis=-1, keepdims=True)
    m1 = lane == i1
    lg2 = jnp.where(m1, neg, lg)
    v2 = jnp.max(lg2, axis=-1, keepdims=True)
    i2 = jnp.min(jnp.where(lg2 == v2, lane, LANES), axis=-1, keepdims=True)
    m2 = lane == i2
    dd = jnp.exp(v2 - v1)
    w1 = 1.0 / (1.0 + dd)
    w2 = dd / (1.0 + dd)

    sel = (m1 | m2).astype(f32)
    rr = lax.broadcasted_iota(jnp.int32, (tm, tm), 0)
    cc = lax.broadcasted_iota(jnp.int32, (tm, tm), 1)
    cum = _dot((rr >= cc).astype(bf16), sel.astype(bf16))
    rank = cum - sel + carry_ref[...]
    r1 = jnp.sum(jnp.where(m1, rank, 0.0), axis=-1, keepdims=True).astype(jnp.int32)
    r2 = jnp.sum(jnp.where(m2, rank, 0.0), axis=-1, keepdims=True).astype(jnp.int32)
    pos_ref[...] = jnp.where(lane == 0, r1, jnp.where(lane == 1, r2, jnp.where(
        lane == 2, i1, jnp.where(lane == 3, i2, 0))))
    gate_ref[...] = jnp.where(lane == 0, w1, jnp.where(lane == 1, w2, 0.0))
    carry_ref[...] = carry_ref[...] + cum[tm - 1:tm, :]
    cnt_ref[...] = carry_ref[...].astype(jnp.int32)


def _router(h, gpre, w_router, *, tm):
    t, d = h.shape
    n_exp = w_router.shape[1]
    wr = jnp.zeros((d, LANES), f32).at[:, :n_exp].set(w_router)
    kern = functools.partial(_router_kernel, tm=tm, d=d, n_exp=n_exp)
    return pl.pallas_call(
        kern,
        grid=(t // tm,),
        in_specs=[
            pl.BlockSpec((tm, d), lambda i: (i, 0)),
            _const_spec((1, d)), _const_spec((d, LANES)),
        ],
        out_specs=[
            pl.BlockSpec((tm * (d // LANES), LANES), lambda i: (i, 0)),
            pl.BlockSpec((tm, LANES), lambda i: (i, 0)),
            pl.BlockSpec((tm, LANES), lambda i: (i, 0)),
            pl.BlockSpec((1, LANES), lambda i: (0, 0)),
        ],
        out_shape=[
            jax.ShapeDtypeStruct((t * (d // LANES), LANES), f32),
            jax.ShapeDtypeStruct((t, LANES), jnp.int32),
            jax.ShapeDtypeStruct((t, LANES), f32),
            jax.ShapeDtypeStruct((1, LANES), jnp.int32),
        ],
        scratch_shapes=[pltpu.VMEM((1, LANES), f32)],
        compiler_params=pltpu.CompilerParams(
            dimension_semantics=("arbitrary",), vmem_limit_bytes=VMEM_LIMIT),
        name="moe_router",
    )(h, gpre.reshape(1, d), wr)


def _dispatch_kernel(cnt_ref, start_ref, nv_ref, pos_ref, up_ref, xs_ref, tile_ref, zero_ref,
                     fsem, sem, zsem, *, tm, te, n_exp, nt, s_n):
    i = pl.program_id(0)
    last = pl.num_programs(0) - 1
    slot = i % RING

    def slab(r, n=1):
        return pl.ds(pl.multiple_of(r * s_n, s_n), n * s_n)

    def fetch(step, sl):
        return pltpu.make_async_copy(up_ref.at[slab(step * tm, tm)], tile_ref.at[sl], fsem.at[sl])

    def drain_step(sl):
        for _ in range(TOP_K):
            pltpu.make_async_copy(tile_ref.at[sl], xs_ref.at[slab(0, tm)], sem.at[sl]).wait()

    @pl.when(i == 0)
    def _():
        fetch(0, 0).start()

    @pl.when(i < last)
    def _():
        fetch(i + 1, (i + 1) % RING).start()

    fetch(i, slot).wait()

    def issue(r, carry):
        for kk in range(TOP_K):
            pltpu.make_async_copy(tile_ref.at[slot, slab(r)],
                                  xs_ref.at[slab(pos_ref[TOP_K * r + kk])],
                                  sem.at[slot]).start(priority=kk)
        return carry

    lax.fori_loop(0, tm, issue, 0, unroll=8)

    @pl.when(i > 0)
    def _():
        drain_step((i + RING - 1) % RING)

    @pl.when(i == last)
    def _():
        drain_step(slot)
        zero_ref[...] = jnp.zeros(zero_ref.shape, zero_ref.dtype)

        def zero_row(dst):
            return pltpu.make_async_copy(zero_ref.at[slab(0)], xs_ref.at[slab(dst)], zsem)

        def zero_tile(ti):
            return pltpu.make_async_copy(zero_ref, xs_ref.at[slab(ti * te, te)], zsem)

        for ex in range(n_exp):
            lo = start_ref[ex] + cnt_ref[ex]
            hi = start_ref[ex] + ((cnt_ref[ex] + te - 1) // te) * te
            lax.fori_loop(lo, hi, lambda r, c: (zero_row(r).start(), c)[1], 0)
            lax.fori_loop(lo, hi, lambda r, c: (zero_row(r).wait(), c)[1], 0)
        lax.fori_loop(nv_ref[0], nt, lambda ti, c: (zero_tile(ti).start(), c)[1], 0)
        lax.fori_loop(nv_ref[0], nt, lambda ti, c: (zero_tile(ti).wait(), c)[1], 0)


def _dispatch(counts, starts, nvalid, pos_flat, upk, *, tm, te, n_exp, nt, s_n):
    t = upk.shape[0] // s_n
    kern = functools.partial(_dispatch_kernel, tm=tm, te=te, n_exp=n_exp, nt=nt, s_n=s_n)
    grid_spec = pltpu.PrefetchScalarGridSpec(
        num_scalar_prefetch=3,
        grid=(t // tm,),
        in_specs=[
            pl.BlockSpec((TOP_K * tm,), lambda i, *_: (i,), memory_space=pltpu.SMEM),
            pl.BlockSpec(memory_space=pl.ANY),
        ],
        out_specs=pl.BlockSpec(memory_space=pl.ANY),
        scratch_shapes=[pltpu.VMEM((RING, tm * s_n, LANES), upk.dtype),
                        pltpu.VMEM((te * s_n, LANES), upk.dtype),
                        pltpu.SemaphoreType.DMA((RING,)), pltpu.SemaphoreType.DMA((RING,)),
                        pltpu.SemaphoreType.DMA],
    )
    return pl.pallas_call(
        kern,
        grid_spec=grid_spec,
        out_shape=jax.ShapeDtypeStruct((nt * te * s_n, LANES), upk.dtype),
        compiler_params=pltpu.CompilerParams(
            dimension_semantics=("arbitrary",), vmem_limit_bytes=VMEM_LIMIT),
        name="moe_dispatch",
    )(counts, starts, nvalid, pos_flat, upk)


def _expert_kernel(te_ref, nv_ref, xs_ref, wg_ref, wu_ref, wd_ref, y_ref, a_ref, acc_ref,
                   *, fc, sc):
    i = pl.program_id(0)
    j = pl.program_id(1)

    @pl.when((i == 0) & (j == 0))
    def _():
        acc_ref[...] = jnp.zeros(acc_ref.shape, f32)

    @pl.when((i >= nv_ref[0]) & (j == 0))
    def _():
        y_ref[...] = jnp.zeros(y_ref.shape, f32)

    @pl.when(i < nv_ref[0])
    def _():
        x = _from_slabs(xs_ref, *acc_ref.shape).astype(bf16)
        for c in range(fc // sc):
            gate = _dot(x, wg_ref[:, c * sc:(c + 1) * sc])
            up = _dot(x, wu_ref[:, c * sc:(c + 1) * sc])
            a_ref[:, c * sc:(c + 1) * sc] = (_silu(gate) * up).astype(bf16)
        part = _dot(a_ref[...], wd_ref[...])
        acc = jnp.where(j == 0, part, acc_ref[...] + part)
        acc_ref[...] = acc
        _to_slabs(y_ref, acc)


def _experts(tile_e, nvalid, xs, w_gu, w_down, *, te, fc, sc):
    n_exp, d, f2 = w_gu.shape
    s_n = d // LANES
    f = f2 // 2
    nf = f // fc
    nt = tile_e.shape[0]

    def jj(i, j, nv):
        return jnp.where(i < nv[0], j, nf - 1)

    grid_spec = pltpu.PrefetchScalarGridSpec(
        num_scalar_prefetch=2,
        grid=(nt, nf),
        in_specs=[
            pl.BlockSpec((te * s_n, LANES), lambda i, j, e, nv: (i, 0)),
            pl.BlockSpec((None, d, fc), lambda i, j, e, nv: (e[i], 0, jj(i, j, nv))),
            pl.BlockSpec((None, d, fc), lambda i, j, e, nv: (e[i], 0, nf + jj(i, j, nv))),
            pl.BlockSpec((None, fc, d), lambda i, j, e, nv: (e[i], jj(i, j, nv), 0)),
        ],
        out_specs=pl.BlockSpec((te * s_n, LANES), lambda i, j, e, nv: (i, 0)),
        scratch_shapes=[pltpu.VMEM((te, fc), bf16), pltpu.VMEM((te, d), f32)],
    )
    kern = functools.partial(_expert_kernel, fc=fc, sc=sc)
    return pl.pallas_call(
        kern,
        grid_spec=grid_spec,
        out_shape=jax.ShapeDtypeStruct(xs.shape, f32),
        compiler_params=pltpu.CompilerParams(
            dimension_semantics=("arbitrary", "arbitrary"), vmem_limit_bytes=VMEM_LIMIT),
        name="moe_experts",
    )(tile_e, nvalid, xs, w_gu, w_gu, w_down)


def _combine_kernel(pos_ref, posn_ref, y_ref, gate_ref, h_ref, gpost_ref, o_ref, buf_ref, sem,
                    *, tm, s_n):
    i = pl.program_id(0)
    slot = i % 2

    def slab(r, n=1):
        return pl.ds(pl.multiple_of(r * s_n, s_n), n * s_n)

    def issue(p_ref, sl):
        def body(r, carry):
            for kk in range(TOP_K):
                pltpu.make_async_copy(y_ref.at[slab(p_ref[TOP_K * r + kk])],
                                      buf_ref.at[sl, kk, slab(r)], sem.at[sl]).start()
            return carry
        lax.fori_loop(0, tm, body, 0, unroll=8)

    @pl.when(i == 0)
    def _():
        issue(pos_ref, 0)

    @pl.when(i + 1 < pl.num_programs(0))
    def _():
        issue(posn_ref, 1 - slot)

    for kk in range(TOP_K):
        pltpu.make_async_copy(y_ref.at[slab(0, tm)], buf_ref.at[slot, kk], sem.at[slot]).wait()
    gates = gate_ref[...]
    mix = (gates[:, 0:1] * _from_slabs(buf_ref.at[slot, 0], *h_ref.shape)
           + gates[:, 1:2] * _from_slabs(buf_ref.at[slot, 1], *h_ref.shape))
    o_ref[...] = h_ref[...] + _rms(mix, gpost_ref[...])


def _combine(pos_flat, y, gates, h, gpost, *, tm):
    t, d = h.shape
    s_n = d // LANES
    kern = functools.partial(_combine_kernel, tm=tm, s_n=s_n)
    n_steps = t // tm
    return pl.pallas_call(
        kern,
        grid=(n_steps,),
        in_specs=[
            pl.BlockSpec((TOP_K * tm,), lambda i: (i,), memory_space=pltpu.SMEM),
            pl.BlockSpec((TOP_K * tm,), lambda i: (jnp.minimum(i + 1, n_steps - 1),),
                         memory_space=pltpu.SMEM),
            pl.BlockSpec(memory_space=pl.ANY),
            pl.BlockSpec((tm, LANES), lambda i: (i, 0)),
            pl.BlockSpec((tm, d), lambda i: (i, 0)),
            _const_spec((1, d)),
        ],
        out_specs=pl.BlockSpec((tm, d), lambda i: (i, 0)),
        out_shape=jax.ShapeDtypeStruct((t, d), f32),
        scratch_shapes=[pltpu.VMEM((2, TOP_K, tm * s_n, LANES), f32),
                        pltpu.SemaphoreType.DMA((2,))],
        compiler_params=pltpu.CompilerParams(
            dimension_semantics=("arbitrary",), vmem_limit_bytes=VMEM_LIMIT),
        name="moe_combine",
    )(pos_flat, pos_flat, y, gates, h, gpost.reshape(1, d))


def _moe(h, gpre, gpost, w_router, w_gu, w_down, *, tm_route, tm_move, te, fc, sc):
    t, d = h.shape
    n_exp = w_router.shape[1]
    upk, pos, gates, counts = _router(h, gpre, w_router, tm=tm_route)
    counts = counts[0, :n_exp]

    nt = (TOP_K * t) // te + n_exp
    tiles_per = (counts + te - 1) // te
    tile_start = jnp.cumsum(tiles_per) - tiles_per
    nvalid = jnp.sum(tiles_per).reshape(1).astype(jnp.int32)
    idx = jnp.minimum(jnp.arange(nt, dtype=jnp.int32), jnp.maximum(nvalid - 1, 0))
    tile_e = (jnp.sum(idx[:, None] >= tile_start[None, :], axis=1) - 1).astype(jnp.int32)
    row_start = (tile_start * te).astype(jnp.int32)
    pos_flat = (pos[:, :TOP_K] + row_start[pos[:, TOP_K:2 * TOP_K]]).reshape(-1)

    xs = _dispatch(counts, row_start, nvalid, pos_flat, upk, tm=tm_move, te=te, n_exp=n_exp, nt=nt,
                   s_n=d // LANES)
    y = _experts(tile_e, nvalid, xs, w_gu, w_down, te=te, fc=fc, sc=sc)
    return _combine(pos_flat, y, gates, h, gpost, tm=tm_move)


def _pick(n, pref):
    return pref if n % pref == 0 else n


def kernel(x, norm_g, conv_w_in, conv_b_in, conv_dw_w, conv_dw_b, conv_ln_g, conv_ln_b, conv_w_out, conv_b_out, hgrn_w_in, hgrn_lower_bounds, hgrn_norm_g, hgrn_w_out, ffn_w_gu, ffn_w_down, moe_router, moe_w_gu, moe_w_down):
    b, l, d = x.shape
    depth = norm_g.shape[0]
    t = b * l
    tm_seq = _pick(l, 512)
    tm_tok = _pick(t, 512)
    h = x
    for i in range(depth):
        j = i // 2
        if i % 2 == 0:
            h = _conv_mixer(h, norm_g[i, 0], norm_g[i, 1], conv_w_in[j].astype(bf16), conv_b_in[j],
                            conv_dw_w[j], conv_dw_b[j], conv_ln_g[j], conv_ln_b[j],
                            conv_w_out[j].astype(bf16), conv_b_out[j], tm=tm_seq)
            f = ffn_w_down.shape[1]
            h = _ffn(h.reshape(t, d), norm_g[i, 2], norm_g[i, 3], ffn_w_gu[j].astype(bf16),
                     ffn_w_down[j].astype(bf16), tm=tm_tok, fc=_pick(f, 256)).reshape(b, l, d)
        else:
            h, w_gu_bf, w_down_bf = _hgrn_mixer(
                h, norm_g[i, 0], norm_g[i, 1], hgrn_w_in[j].astype(bf16), hgrn_lower_bounds,
                hgrn_norm_g[j], hgrn_w_out[j].astype(bf16), moe_w_gu[j], moe_w_down[j],
                tm=tm_seq, layer=i)
            f = moe_w_down.shape[2]
            fc = f // 2 if (f // 2) % LANES == 0 else f
            h = _moe(h.reshape(t, d), norm_g[i, 2], norm_g[i, 3], moe_router[j],
                     w_gu_bf, w_down_bf,
                     tm_route=tm_tok, tm_move=_pick(t, 256), te=_pick(t, 512), fc=fc,
                     sc=_pick(fc, 256)).reshape(b, l, d)
    return h
```

```python
import functools

import jax
import jax.numpy as jnp
from jax import lax
from jax.experimental import pallas as pl
from jax.experimental.pallas import tpu as pltpu

EPS = 1e-6
CHUNK = 64
SUB = 16
HEAD = 128
CONV_WIDTH = 31
HALO = 32
CONV_ROWS = 64
LN_ROWS = 32
ROW_PITCH = 2
LANES = 128
SUBLANES = 8
TOP_K = 2
RING = 3
EXP_CLAMP = 80.0
VMEM_LIMIT = 56 * 1024 * 1024

f32 = jnp.float32
bf16 = jnp.bfloat16


def _dot(a, b):
    return jnp.dot(a, b, preferred_element_type=f32)


def _dot_nt(a, b):
    return lax.dot_general(a, b, (((1,), (1,)), ((), ())), preferred_element_type=f32)


def _dot_tn(a, b):
    return lax.dot_general(a, b, (((0,), (0,)), ((), ())), preferred_element_type=f32)


def _rms(x, g):
    return x * lax.rsqrt(jnp.mean(x * x, axis=-1, keepdims=True) + EPS) * g


def _silu(x):
    return x * jax.nn.sigmoid(x)


def _to_slabs(ref, x):
    n, d = x.shape
    s_n = d // LANES
    for s in range(s_n):
        ref[pl.ds(s, n, stride=s_n), :] = x[:, s * LANES:(s + 1) * LANES]


def _from_slabs(ref, n, d):
    s_n = d // LANES
    return jnp.concatenate([ref[pl.ds(s, n, stride=s_n), :] for s in range(s_n)], axis=1)


def _split_bf16(x):
    hi = x.astype(bf16)
    return hi, (x - hi.astype(f32)).astype(bf16)


def _const_spec(shape):
    nd = len(shape)
    return pl.BlockSpec(shape, lambda *_: (0,) * nd, pipeline_mode=pl.Buffered(1))


def _conv_mixer_kernel(x_ref, gpre_ref, gpost_ref, win_ref, bin_ref, dww_ref, dwb_ref,
                       lng_ref, lnb_ref, wout_ref, bout_ref, o_ref, gl_ref, y_ref, z_ref,
                       *, tm, d):
    i = pl.program_id(1)
    nslab = d // LANES

    def rows(start, n):
        return pl.ds(ROW_PITCH * start, n, stride=ROW_PITCH)

    @pl.when(i == 0)
    def _():
        for c in range(nslab):
            gl_ref[c, rows(0, HALO), :] = jnp.zeros((HALO, LANES), f32)

    @pl.when(i > 0)
    def _():
        for c in range(nslab):
            gl_ref[c, rows(0, HALO), :] = gl_ref[c, rows(tm, HALO), :]

    x = x_ref[...]
    u = _rms(x, gpre_ref[...]).astype(bf16)
    a = _dot(u, win_ref[:, :d]) + bin_ref[:, :d]
    gate = _dot(u, win_ref[:, d:]) + bin_ref[:, d:]
    glu = a * jax.nn.sigmoid(gate)
    for c in range(nslab):
        gl_ref[c, rows(HALO, tm), :] = glu[:, c * LANES:(c + 1) * LANES]

    off = HALO - (CONV_WIDTH - 1)
    for c in range(nslab):
        cols = slice(c * LANES, (c + 1) * LANES)
        for rb in range(tm // CONV_ROWS):
            base = rb * CONV_ROWS
            acc = jnp.broadcast_to(dwb_ref[:, cols], (CONV_ROWS, LANES))
            for j in range(CONV_WIDTH):
                acc = acc + dww_ref[j:j + 1, cols] * gl_ref[c, rows(base + off + j, CONV_ROWS), :]
            y_ref[base:base + CONV_ROWS, cols] = acc

    for rb in range(tm // LN_ROWS):
        base = rb * LN_ROWS
        acc = y_ref[base:base + LN_ROWS, :]
        mu = jnp.mean(acc, axis=-1, keepdims=True)
        xc = acc - mu
        y = xc * lax.rsqrt(jnp.mean(xc * xc, axis=-1, keepdims=True) + EPS)
        y = y * lng_ref[...] + lnb_ref[...]
        z_ref[base:base + LN_ROWS, :] = _silu(y).astype(bf16)

    out = _dot(z_ref[...], wout_ref[...]) + bout_ref[...]
    o_ref[...] = x + _rms(out, gpost_ref[...])


def _conv_mixer(x, gpre, gpost, w_in, b_in, dw_w, dw_b, ln_g, ln_b, w_out, b_out, *, tm):
    b, l, d = x.shape
    row = lambda v: v.reshape(1, -1)
    kern = functools.partial(_conv_mixer_kernel, tm=tm, d=d)
    return pl.pallas_call(
        kern,
        grid=(b, l // tm),
        in_specs=[
            pl.BlockSpec((None, tm, d), lambda bi, i: (bi, i, 0)),
            _const_spec((1, d)), _const_spec((1, d)),
            _const_spec((d, 2 * d)), _const_spec((1, 2 * d)),
            _const_spec((CONV_WIDTH, d)), _const_spec((1, d)),
            _const_spec((1, d)), _const_spec((1, d)),
            _const_spec((d, d)), _const_spec((1, d)),
        ],
        out_specs=pl.BlockSpec((None, tm, d), lambda bi, i: (bi, i, 0)),
        out_shape=jax.ShapeDtypeStruct((b, l, d), f32),
        scratch_shapes=[pltpu.VMEM((d // LANES, ROW_PITCH * (tm + HALO), LANES), f32),
                        pltpu.VMEM((tm, d), f32), pltpu.VMEM((tm, d), bf16)],
        compiler_params=pltpu.CompilerParams(
            dimension_semantics=("arbitrary", "arbitrary"), vmem_limit_bytes=VMEM_LIMIT),
        name="conv_mixer",
    )(x, row(gpre), row(gpost), w_in, row(b_in), dw_w, row(dw_b), row(ln_g), row(ln_b),
      w_out, row(b_out))


def _ffn_kernel(h_ref, gpre_ref, gpost_ref, wgu_ref, wd_ref, o_ref, a_ref, *, f, fc):
    x = h_ref[...]
    u = _rms(x, gpre_ref[...]).astype(bf16)
    for c in range(f // fc):
        gate = _dot(u, wgu_ref[:, c * fc:(c + 1) * fc])
        up = _dot(u, wgu_ref[:, f + c * fc:f + (c + 1) * fc])
        a_ref[:, c * fc:(c + 1) * fc] = (_silu(gate) * up).astype(bf16)
    out = _dot(a_ref[...], wd_ref[...])
    o_ref[...] = x + _rms(out, gpost_ref[...])


def _ffn(h, gpre, gpost, w_gu, w_down, *, tm, fc):
    t, d = h.shape
    f = w_down.shape[0]
    kern = functools.partial(_ffn_kernel, f=f, fc=fc)
    return pl.pallas_call(
        kern,
        grid=(t // tm,),
        in_specs=[
            pl.BlockSpec((tm, d), lambda i: (i, 0)),
            _const_spec((1, d)), _const_spec((1, d)),
            _const_spec((d, 2 * f)), _const_spec((f, d)),
        ],
        out_specs=pl.BlockSpec((tm, d), lambda i: (i, 0)),
        out_shape=jax.ShapeDtypeStruct((t, d), f32),
        scratch_shapes=[pltpu.VMEM((tm, f), bf16)],
        compiler_params=pltpu.CompilerParams(
            dimension_semantics=("arbitrary",), vmem_limit_bytes=VMEM_LIMIT),
        name="dense_ffn",
    )(h, gpre.reshape(1, d), gpost.reshape(1, d), w_gu, w_down)


def _hgrn_kernel(h_ref, gpre_ref, gpost_ref, win_ref, lbp_ref, ng_ref, wout_ref, wgu_ref, wdn_ref,
                 o_ref, wgu_bf_ref, wdn_bf_ref,
                 q_ref, k_ref, v_ref, g_ref, og_ref, oh_ref, st_ref, *, tm, d, layer):
    i = pl.program_id(1)
    heads = d // HEAD
    nsub = CHUNK // SUB

    wgu_bf_ref[...] = wgu_ref[...].astype(bf16)
    wdn_bf_ref[...] = wdn_ref[...].astype(bf16)

    @pl.when(i == 0)
    def _():
        st_ref[...] = jnp.zeros(st_ref.shape, f32)

    x = h_ref[...]
    u = _rms(x, gpre_ref[...]).astype(bf16)

    lbp = lbp_ref[...]
    e = jnp.exp(lbp - jnp.max(lbp, axis=0, keepdims=True))
    sm = e / jnp.sum(e, axis=0, keepdims=True)
    lb = jnp.sum(sm[1:layer + 1, :], axis=0, keepdims=True)

    q_ref[...] = _silu(_dot(u, win_ref[:, 0:d]))
    forget = lb + (1.0 - lb) * jax.nn.sigmoid(_dot(u, win_ref[:, d:2 * d]))
    k_ref[...] = 1.0 - forget
    g_ref[...] = jnp.log(forget)
    v_ref[...] = _dot(u, win_ref[:, 2 * d:3 * d]).astype(bf16)
    og_ref[...] = _silu(_dot(u, win_ref[:, 3 * d:4 * d]))

    rr = lax.broadcasted_iota(jnp.int32, (CHUNK, CHUNK), 0)
    cc = lax.broadcasted_iota(jnp.int32, (CHUNK, CHUNK), 1)
    tri = (rr >= cc).astype(bf16)
    diag_mask = (rr >= cc) & ((rr // SUB) == (cc // SUB))
    ng = ng_ref[...]
    zeros_blk = jnp.zeros((SUB, d), f32)

    def block_rows(vals):
        return jnp.concatenate([jnp.broadcast_to(r, (SUB, d)) for r in vals], axis=0)

    def chunk_body(c, carry):
        r0 = pl.multiple_of(c * CHUNK, CHUNK)
        rows = pl.ds(r0, CHUNK)
        g_hi, g_lo = _split_bf16(g_ref[rows, :])
        big_g = _dot(tri, g_hi) + _dot(tri, g_lo)
        q = q_ref[rows, :]
        k = k_ref[rows, :]

        ends = [big_g[SUB * b + SUB - 1:SUB * b + SUB, :] for b in range(nsub)]
        prevs = [jnp.zeros((1, d), f32)] + ends[:-1]
        ref_prev = block_rows(prevs)
        ref_end = block_rows(ends)
        qw = q * jnp.exp(big_g - ref_prev)
        k_end = k * jnp.exp(ref_end - big_g)
        qe = (qw * block_rows([jnp.exp(p) for p in prevs])).astype(bf16)
        k_dec = (k_end * block_rows([jnp.exp(ends[-1] - e) for e in ends])).astype(bf16)
        half = 0.5 * (ref_prev - ref_end)
        q_mid = (q * jnp.exp(jnp.clip(big_g - ref_prev + half, -EXP_CLAMP, EXP_CLAMP))).astype(bf16)
        k_mid = (k * jnp.exp(jnp.clip(ref_end - big_g + half, -EXP_CLAMP, EXP_CLAMP))).astype(bf16)
        q_parts, k_parts = [], []
        for jb in range(nsub - 1):
            qp = [zeros_blk] * (jb + 1)
            for b in range(jb + 1, nsub):
                piece = qw[SUB * b:SUB * (b + 1), :]
                if b > jb + 1:
                    piece = piece * jnp.exp(prevs[b] - ends[jb])
                qp.append(piece)
            q_parts.append(jnp.concatenate(qp, axis=0).astype(bf16))
            kp = [zeros_blk] * nsub
            kp[jb] = k_end[SUB * jb:SUB * (jb + 1), :]
            k_parts.append(jnp.concatenate(kp, axis=0).astype(bf16))
        decay = jnp.exp(ends[-1])

        for hd in range(heads):
            sl = slice(hd * HEAD, (hd + 1) * HEAD)
            v = v_ref[rows, sl]
            st = st_ref[hd]
            s_off = _dot_nt(jnp.concatenate([p[:, sl] for p in q_parts], axis=1),
                            jnp.concatenate([p[:, sl] for p in k_parts], axis=1))
            s_diag = _dot_nt(q_mid[:, sl], k_mid[:, sl])
            scores = (s_off + jnp.where(diag_mask, s_diag, 0.0)).astype(bf16)
            o = _dot_nt(qe[:, sl], st.astype(bf16)) + _dot(scores, v)
            st_ref[hd] = decay[:, sl] * st + _dot_tn(v, k_dec[:, sl])
            o = o * lax.rsqrt(jnp.mean(o * o, axis=-1, keepdims=True) + EPS) * ng
            oh_ref[rows, sl] = (o * og_ref[rows, sl]).astype(bf16)
        return carry

    lax.fori_loop(0, tm // CHUNK, chunk_body, 0, unroll=True)
    out = _dot(oh_ref[...], wout_ref[...])
    o_ref[...] = x + _rms(out, gpost_ref[...])


def _hgrn_mixer(h, gpre, gpost, w_in, lower_bounds, norm_g, w_out, moe_w_gu, moe_w_down,
                *, tm, layer):
    b, l, d = h.shape
    depth = lower_bounds.shape[0]
    n_tiles = l // tm
    steps = b * n_tiles
    wgu = moe_w_gu.reshape(-1, moe_w_gu.shape[-1])
    wdn = moe_w_down.reshape(-1, moe_w_down.shape[-1])
    gu_rows, dn_rows = wgu.shape[0] // steps, wdn.shape[0] // steps
    assert gu_rows * steps == wgu.shape[0] and dn_rows * steps == wdn.shape[0]
    slice_map = lambda bi, i: (bi * n_tiles + i, 0)
    kern = functools.partial(_hgrn_kernel, tm=tm, d=d, layer=layer)
    out, wgu_bf, wdn_bf = pl.pallas_call(
        kern,
        grid=(b, n_tiles),
        in_specs=[
            pl.BlockSpec((None, tm, d), lambda bi, i: (bi, i, 0)),
            _const_spec((1, d)), _const_spec((1, d)),
            _const_spec((d, 4 * d)), _const_spec((depth, d)), _const_spec((1, HEAD)),
            _const_spec((d, d)),
            pl.BlockSpec((gu_rows, wgu.shape[1]), slice_map),
            pl.BlockSpec((dn_rows, wdn.shape[1]), slice_map),
        ],
        out_specs=[
            pl.BlockSpec((None, tm, d), lambda bi, i: (bi, i, 0)),
            pl.BlockSpec((gu_rows, wgu.shape[1]), slice_map),
            pl.BlockSpec((dn_rows, wdn.shape[1]), slice_map),
        ],
        out_shape=[
            jax.ShapeDtypeStruct((b, l, d), f32),
            jax.ShapeDtypeStruct(wgu.shape, bf16),
            jax.ShapeDtypeStruct(wdn.shape, bf16),
        ],
        scratch_shapes=[pltpu.VMEM((tm, d), f32), pltpu.VMEM((tm, d), f32),
                        pltpu.VMEM((tm, d), bf16), pltpu.VMEM((tm, d), f32),
                        pltpu.VMEM((tm, d), f32), pltpu.VMEM((tm, d), bf16),
                        pltpu.VMEM((d // HEAD, HEAD, HEAD), f32)],
        compiler_params=pltpu.CompilerParams(
            dimension_semantics=("arbitrary", "arbitrary"), vmem_limit_bytes=VMEM_LIMIT),
        name="hgrn_mixer",
    )(h, gpre.reshape(1, d), gpost.reshape(1, d), w_in, lower_bounds, norm_g.reshape(1, HEAD), w_out,
      wgu, wdn)
    return out, wgu_bf.reshape(moe_w_gu.shape), wdn_bf.reshape(moe_w_down.shape)


def _router_kernel(h_ref, gpre_ref, wr_ref, up_ref, pos_ref, gate_ref, cnt_ref, carry_ref,
                   *, tm, d, n_exp):
    i = pl.program_id(0)

    @pl.when(i == 0)
    def _():
        carry_ref[...] = jnp.zeros(carry_ref.shape, f32)

    u = _rms(h_ref[...], gpre_ref[...])
    _to_slabs(up_ref, u)

    u_hi, u_lo = _split_bf16(u)
    w_hi, w_lo = _split_bf16(wr_ref[...])
    logits = _dot(u_hi, w_hi) + (_dot(u_hi, w_lo) + _dot(u_lo, w_hi))
    lane = lax.broadcasted_iota(jnp.int32, (tm, LANES), 1)
    neg = jnp.float32(-jnp.inf)
    lg = jnp.where(lane < n_exp, logits, neg)
    v1 = jnp.max(lg, axis=-1, keepdims=True)
    i1 = jnp.min(jnp.where(lg == v1, lane, LANES), axis=-1, keepdims=True)
    m1 = lane == i1
    lg2 = jnp.where(m1, neg, lg)
    v2 = jnp.max(lg2, axis=-1, keepdims=True)
    i2 = jnp.min(jnp.where(lg2 == v2, lane, LANES), axis=-1, keepdims=True)
    m2 = lane == i2
    dd = jnp.exp(v2 - v1)
    w1 = 1.0 / (1.0 + dd)
    w2 = dd / (1.0 + dd)

    sel = (m1 | m2).astype(f32)
    rr = lax.broadcasted_iota(jnp.int32, (tm, tm), 0)
    cc = lax.broadcasted_iota(jnp.int32, (tm, tm), 1)
    cum = _dot((rr >= cc).astype(bf16), sel.astype(bf16))
    rank = cum - sel + carry_ref[...]
    r1 = jnp.sum(jnp.where(m1, rank, 0.0), axis=-1, keepdims=True).astype(jnp.int32)
    r2 = jnp.sum(jnp.where(m2, rank, 0.0), axis=-1, keepdims=True).astype(jnp.int32)
    pos_ref[...] = jnp.where(lane == 0, r1, jnp.where(lane == 1, r2, jnp.where(
        lane == 2, i1, jnp.where(lane == 3, i2, 0))))
    gate_ref[...] = jnp.where(lane == 0, w1, jnp.where(lane == 1, w2, 0.0))
    carry_ref[...] = carry_ref[...] + cum[tm - 1:tm, :]
    cnt_ref[...] = carry_ref[...].astype(jnp.int32)


def _router(h, gpre, w_router, *, tm):
    t, d = h.shape
    n_exp = w_router.shape[1]
    wr = jnp.zeros((d, LANES), f32).at[:, :n_exp].set(w_router)
    kern = functools.partial(_router_kernel, tm=tm, d=d, n_exp=n_exp)
    return pl.pallas_call(
        kern,
        grid=(t // tm,),
        in_specs=[
            pl.BlockSpec((tm, d), lambda i: (i, 0)),
            _const_spec((1, d)), _const_spec((d, LANES)),
        ],
        out_specs=[
            pl.BlockSpec((tm * (d // LANES), LANES), lambda i: (i, 0)),
            pl.BlockSpec((tm, LANES), lambda i: (i, 0)),
            pl.BlockSpec((tm, LANES), lambda i: (i, 0)),
            pl.BlockSpec((1, LANES), lambda i: (0, 0)),
        ],
        out_shape=[
            jax.ShapeDtypeStruct((t * (d // LANES), LANES), f32),
            jax.ShapeDtypeStruct((t, LANES), jnp.int32),
            jax.ShapeDtypeStruct((t, LANES), f32),
            jax.ShapeDtypeStruct((1, LANES), jnp.int32),
        ],
        scratch_shapes=[pltpu.VMEM((1, LANES), f32)],
        compiler_params=pltpu.CompilerParams(
            dimension_semantics=("arbitrary",), vmem_limit_bytes=VMEM_LIMIT),
        name="moe_router",
    )(h, gpre.reshape(1, d), wr)


def _dispatch_kernel(cnt_ref, start_ref, nv_ref, pos_ref, up_ref, xs_ref, tile_ref, zero_ref,
                     fsem, sem, zsem, *, tm, te, n_exp, nt, s_n):
    i = pl.program_id(0)
    last = pl.num_programs(0) - 1
    slot = i % RING

    def slab(r, n=1):
        return pl.ds(pl.multiple_of(r * s_n, s_n), n * s_n)

    def fetch(step, sl):
        return pltpu.make_async_copy(up_ref.at[slab(step * tm, tm)], tile_ref.at[sl], fsem.at[sl])

    def drain_step(sl):
        for _ in range(TOP_K):
            pltpu.make_async_copy(tile_ref.at[sl], xs_ref.at[slab(0, tm)], sem.at[sl]).wait()

    @pl.when(i == 0)
    def _():
        fetch(0, 0).start()

    @pl.when(i < last)
    def _():
        fetch(i + 1, (i + 1) % RING).start()

    fetch(i, slot).wait()

    def issue(r, carry):
        for kk in range(TOP_K):
            pltpu.make_async_copy(tile_ref.at[slot, slab(r)],
                                  xs_ref.at[slab(pos_ref[TOP_K * r + kk])],
                                  sem.at[slot]).start(priority=kk)
        return carry

    lax.fori_loop(0, tm, issue, 0, unroll=8)

    @pl.when(i > 0)
    def _():
        drain_step((i + RING - 1) % RING)

    @pl.when(i == last)
    def _():
        drain_step(slot)
        zero_ref[...] = jnp.zeros(zero_ref.shape, zero_ref.dtype)

        def zero_row(dst):
            return pltpu.make_async_copy(zero_ref.at[slab(0)], xs_ref.at[slab(dst)], zsem)

        def zero_tile(ti):
            return pltpu.make_async_copy(zero_ref, xs_ref.at[slab(ti * te, te)], zsem)

        for ex in range(n_exp):
            lo = start_ref[ex] + cnt_ref[ex]
            hi = start_ref[ex] + ((cnt_ref[ex] + te - 1) // te) * te
            lax.fori_loop(lo, hi, lambda r, c: (zero_row(r).start(), c)[1], 0)
            lax.fori_loop(lo, hi, lambda r, c: (zero_row(r).wait(), c)[1], 0)
        lax.fori_loop(nv_ref[0], nt, lambda ti, c: (zero_tile(ti).start(), c)[1], 0)
        lax.fori_loop(nv_ref[0], nt, lambda ti, c: (zero_tile(ti).wait(), c)[1], 0)


def _dispatch(counts, starts, nvalid, pos_flat, upk, *, tm, te, n_exp, nt, s_n):
    t = upk.shape[0] // s_n
    kern = functools.partial(_dispatch_kernel, tm=tm, te=te, n_exp=n_exp, nt=nt, s_n=s_n)
    grid_spec = pltpu.PrefetchScalarGridSpec(
        num_scalar_prefetch=3,
        grid=(t // tm,),
        in_specs=[
            pl.BlockSpec((TOP_K * tm,), lambda i, *_: (i,), memory_space=pltpu.SMEM),
            pl.BlockSpec(memory_space=pl.ANY),
        ],
        out_specs=pl.BlockSpec(memory_space=pl.ANY),
        scratch_shapes=[pltpu.VMEM((RING, tm * s_n, LANES), upk.dtype),
                        pltpu.VMEM((te * s_n, LANES), upk.dtype),
                        pltpu.SemaphoreType.DMA((RING,)), pltpu.SemaphoreType.DMA((RING,)),
                        pltpu.SemaphoreType.DMA],
    )
    return pl.pallas_call(
        kern,
        grid_spec=grid_spec,
        out_shape=jax.ShapeDtypeStruct((nt * te * s_n, LANES), upk.dtype),
        compiler_params=pltpu.CompilerParams(
            dimension_semantics=("arbitrary",), vmem_limit_bytes=VMEM_LIMIT),
        name="moe_dispatch",
    )(counts, starts, nvalid, pos_flat, upk)


def _expert_kernel(te_ref, nv_ref, xs_ref, wg_ref, wu_ref, wd_ref, y_ref, a_ref, acc_ref,
                   *, fc, sc):
    i = pl.program_id(0)
    j = pl.program_id(1)

    @pl.when((i == 0) & (j == 0))
    def _():
        acc_ref[...] = jnp.zeros(acc_ref.shape, f32)

    @pl.when((i >= nv_ref[0]) & (j == 0))
    def _():
        y_ref[...] = jnp.zeros(y_ref.shape, f32)

    @pl.when(i < nv_ref[0])
    def _():
        x = _from_slabs(xs_ref, *acc_ref.shape).astype(bf16)
        for c in range(fc // sc):
            gate = _dot(x, wg_ref[:, c * sc:(c + 1) * sc])
            up = _dot(x, wu_ref[:, c * sc:(c + 1) * sc])
            a_ref[:, c * sc:(c + 1) * sc] = (_silu(gate) * up).astype(bf16)
        part = _dot(a_ref[...], wd_ref[...])
        acc = jnp.where(j == 0, part, acc_ref[...] + part)
        acc_ref[...] = acc
        _to_slabs(y_ref, acc)


def _experts(tile_e, nvalid, xs, w_gu, w_down, *, te, fc, sc):
    n_exp, d, f2 = w_gu.shape
    s_n = d // LANES
    f = f2 // 2
    nf = f // fc
    nt = tile_e.shape[0]

    def jj(i, j, nv):
        return jnp.where(i < nv[0], j, nf - 1)

    grid_spec = pltpu.PrefetchScalarGridSpec(
        num_scalar_prefetch=2,
        grid=(nt, nf),
        in_specs=[
            pl.BlockSpec((te * s_n, LANES), lambda i, j, e, nv: (i, 0)),
            pl.BlockSpec((None, d, fc), lambda i, j, e, nv: (e[i], 0, jj(i, j, nv))),
            pl.BlockSpec((None, d, fc), lambda i, j, e, nv: (e[i], 0, nf + jj(i, j, nv))),
            pl.BlockSpec((None, fc, d), lambda i, j, e, nv: (e[i], jj(i, j, nv), 0)),
        ],
        out_specs=pl.BlockSpec((te * s_n, LANES), lambda i, j, e, nv: (i, 0)),
        scratch_shapes=[pltpu.VMEM((te, fc), bf16), pltpu.VMEM((te, d), f32)],
    )
    kern = functools.partial(_expert_kernel, fc=fc, sc=sc)
    return pl.pallas_call(
        kern,
        grid_spec=grid_spec,
        out_shape=jax.ShapeDtypeStruct(xs.shape, f32),
        compiler_params=pltpu.CompilerParams(
            dimension_semantics=("arbitrary", "arbitrary"), vmem_limit_bytes=VMEM_LIMIT),
        name="moe_experts",
    )(tile_e, nvalid, xs, w_gu, w_gu, w_down)


def _combine_kernel(pos_ref, posn_ref, y_ref, gate_ref, h_ref, gpost_ref, o_ref, buf_ref, sem,
                    *, tm, s_n):
    i = pl.program_id(0)
    slot = i % 2

    def slab(r, n=1):
        return pl.ds(pl.multiple_of(r * s_n, s_n), n * s_n)

    def issue(p_ref, sl):
        def body(r, carry):
            for kk in range(TOP_K):
                pltpu.make_async_copy(y_ref.at[slab(p_ref[TOP_K * r + kk])],
                                      buf_ref.at[sl, kk, slab(r)], sem.at[sl]).start()
            return carry
        lax.fori_loop(0, tm, body, 0, unroll=8)

    @pl.when(i == 0)
    def _():
        issue(pos_ref, 0)

    @pl.when(i + 1 < pl.num_programs(0))
    def _():
        issue(posn_ref, 1 - slot)

    for kk in range(TOP_K):
        pltpu.make_async_copy(y_ref.at[slab(0, tm)], buf_ref.at[slot, kk], sem.at[slot]).wait()
    gates = gate_ref[...]
    mix = (gates[:, 0:1] * _from_slabs(buf_ref.at[slot, 0], *h_ref.shape)
           + gates[:, 1:2] * _from_slabs(buf_ref.at[slot, 1], *h_ref.shape))
    o_ref[...] = h_ref[...] + _rms(mix, gpost_ref[...])


def _combine(pos_flat, y, gates, h, gpost, *, tm):
    t, d = h.shape
    s_n = d // LANES
    kern = functools.partial(_combine_kernel, tm=tm, s_n=s_n)
    n_steps = t // tm
    return pl.pallas_call(
        kern,
        grid=(n_steps,),
        in_specs=[
            pl.BlockSpec((TOP_K * tm,), lambda i: (i,), memory_space=pltpu.SMEM),
            pl.BlockSpec((TOP_K * tm,), lambda i: (jnp.minimum(i + 1, n_steps - 1),),
                         memory_space=pltpu.SMEM),
            pl.BlockSpec(memory_space=pl.ANY),
            pl.BlockSpec((tm, LANES), lambda i: (i, 0)),
            pl.BlockSpec((tm, d), lambda i: (i, 0)),
            _const_spec((1, d)),
        ],
        out_specs=pl.BlockSpec((tm, d), lambda i: (i, 0)),
        out_shape=jax.ShapeDtypeStruct((t, d), f32),
        scratch_shapes=[pltpu.VMEM((2, TOP_K, tm * s_n, LANES), f32),
                        pltpu.SemaphoreType.DMA((2,))],
        compiler_params=pltpu.CompilerParams(
            dimension_semantics=("arbitrary",), vmem_limit_bytes=VMEM_LIMIT),
        name="moe_combine",
    )(pos_flat, pos_flat, y, gates, h, gpost.reshape(1, d))


def _moe(h, gpre, gpost, w_router, w_gu, w_down, *, tm_route, tm_move, te, fc, sc):
    t, d = h.shape
    n_exp = w_router.shape[1]
    upk, pos, gates, counts = _router(h, gpre, w_router, tm=tm_route)
    counts = counts[0, :n_exp]

    nt = (TOP_K * t) // te + n_exp
    tiles_per = (counts + te - 1) // te
    tile_start = jnp.cumsum(tiles_per) - tiles_per
    nvalid = jnp.sum(tiles_per).reshape(1).astype(jnp.int32)
    idx = jnp.minimum(jnp.arange(nt, dtype=jnp.int32), jnp.maximum(nvalid - 1, 0))
    tile_e = (jnp.sum(idx[:, None] >= tile_start[None, :], axis=1) - 1).astype(jnp.int32)
    row_start = (tile_start * te).astype(jnp.int32)
    pos_flat = (pos[:, :TOP_K] + row_start[pos[:, TOP_K:2 * TOP_K]]).reshape(-1)

    xs = _dispatch(counts, row_start, nvalid, pos_flat, upk, tm=tm_move, te=te, n_exp=n_exp, nt=nt,
                   s_n=d // LANES)
    y = _experts(tile_e, nvalid, xs, w_gu, w_down, te=te, fc=fc, sc=sc)
    return _combine(pos_flat, y, gates, h, gpost, tm=tm_move)


def _pick(n, pref):
    return pref if n % pref == 0 else n


def kernel(x, norm_g, conv_w_in, conv_b_in, conv_dw_w, conv_dw_b, conv_ln_g, conv_ln_b, conv_w_out, conv_b_out, hgrn_w_in, hgrn_lower_bounds, hgrn_norm_g, hgrn_w_out, ffn_w_gu, ffn_w_down, moe_router, moe_w_gu, moe_w_down):
    b, l, d = x.shape
    depth = norm_g.shape[0]
    t = b * l
    tm_seq = _pick(l, 512)
    tm_tok = _pick(t, 512)
    h = x
    for i in range(depth):
        j = i // 2
        if i % 2 == 0:
            h = _conv_mixer(h, norm_g[i, 0], norm_g[i, 1], conv_w_in[j].astype(bf16), conv_b_in[j],
                            conv_dw_w[j], conv_dw_b[j], conv_ln_g[j], conv_ln_b[j],
                            conv_w_out[j].astype(bf16), conv_b_out[j], tm=tm_seq)
            f = ffn_w_down.shape[1]
            h = _ffn(h.reshape(t, d), norm_g[i, 2], norm_g[i, 3], ffn_w_gu[j].astype(bf16),
                     ffn_w_down[j].astype(bf16), tm=tm_tok, fc=_pick(f, 256)).reshape(b, l, d)
        else:
            h, w_gu_bf, w_down_bf = _hgrn_mixer(
                h, norm_g[i, 0], norm_g[i, 1], hgrn_w_in[j].astype(bf16), hgrn_lower_bounds,
                hgrn_norm_g[j], hgrn_w_out[j].astype(bf16), moe_w_gu[j], moe_w_down[j],
                tm=tm_seq, layer=i)
            f = moe_w_down.shape[2]
            fc = f // 2 if (f // 2) % LANES == 0 else f
            h = _moe(h.reshape(t, d), norm_g[i, 2], norm_g[i, 3], moe_router[j],
                     w_gu_bf, w_down_bf,
                     tm_route=tm_tok, tm_move=tm_tok, te=_pick(t, 512), fc=fc,
                     sc=_pick(fc, 256)).reshape(b, l, d)
    return h
```

```python
import functools

import jax
import jax.numpy as jnp
from jax import lax
from jax.experimental import pallas as pl
from jax.experimental.pallas import tpu as pltpu

EPS = 1e-6
CHUNK = 64
SUB = 16
HEAD = 128
CONV_WIDTH = 31
HALO = 32
CONV_ROWS = 64
LN_ROWS = 32
ROW_PITCH = 2
LANES = 128
SUBLANES = 8
TOP_K = 2
ROUTE_LANES = 8
RING = 3
EXP_CLAMP = 80.0
VMEM_LIMIT = 56 * 1024 * 1024

f32 = jnp.float32
bf16 = jnp.bfloat16


def _dot(a, b):
    return jnp.dot(a, b, preferred_element_type=f32)


def _dot_nt(a, b):
    return lax.dot_general(a, b, (((1,), (1,)), ((), ())), preferred_element_type=f32)


def _dot_tn(a, b):
    return lax.dot_general(a, b, (((0,), (0,)), ((), ())), preferred_element_type=f32)


def _rms(x, g):
    return x * lax.rsqrt(jnp.mean(x * x, axis=-1, keepdims=True) + EPS) * g


def _silu(x):
    return x * jax.nn.sigmoid(x)


def _to_slabs(ref, x):
    n, d = x.shape
    s_n = d // LANES
    for s in range(s_n):
        ref[pl.ds(s, n, stride=s_n), :] = x[:, s * LANES:(s + 1) * LANES]


def _from_slabs(ref, n, d):
    s_n = d // LANES
    return jnp.concatenate([ref[pl.ds(s, n, stride=s_n), :] for s in range(s_n)], axis=1)


def _split_bf16(x):
    hi = x.astype(bf16)
    return hi, (x - hi.astype(f32)).astype(bf16)


def _const_spec(shape):
    nd = len(shape)
    return pl.BlockSpec(shape, lambda *_: (0,) * nd, pipeline_mode=pl.Buffered(1))


def _conv_mixer_kernel(x_ref, gpre_ref, gpost_ref, win_ref, bin_ref, dww_ref, dwb_ref,
                       lng_ref, lnb_ref, wout_ref, bout_ref, o_ref, gl_ref, y_ref, z_ref,
                       *, tm, d):
    i = pl.program_id(1)
    nslab = d // LANES

    def rows(start, n):
        return pl.ds(ROW_PITCH * start, n, stride=ROW_PITCH)

    @pl.when(i == 0)
    def _():
        for c in range(nslab):
            gl_ref[c, rows(0, HALO), :] = jnp.zeros((HALO, LANES), f32)

    @pl.when(i > 0)
    def _():
        for c in range(nslab):
            gl_ref[c, rows(0, HALO), :] = gl_ref[c, rows(tm, HALO), :]

    x = x_ref[...]
    u = _rms(x, gpre_ref[...]).astype(bf16)
    a = _dot(u, win_ref[:, :d]) + bin_ref[:, :d]
    gate = _dot(u, win_ref[:, d:]) + bin_ref[:, d:]
    glu = a * jax.nn.sigmoid(gate)
    for c in range(nslab):
        gl_ref[c, rows(HALO, tm), :] = glu[:, c * LANES:(c + 1) * LANES]

    off = HALO - (CONV_WIDTH - 1)
    for c in range(nslab):
        cols = slice(c * LANES, (c + 1) * LANES)
        for rb in range(tm // CONV_ROWS):
            base = rb * CONV_ROWS
            acc = jnp.broadcast_to(dwb_ref[:, cols], (CONV_ROWS, LANES))
            for j in range(CONV_WIDTH):
                acc = acc + dww_ref[j:j + 1, cols] * gl_ref[c, rows(base + off + j, CONV_ROWS), :]
            y_ref[base:base + CONV_ROWS, cols] = acc

    for rb in range(tm // LN_ROWS):
        base = rb * LN_ROWS
        acc = y_ref[base:base + LN_ROWS, :]
        mu = jnp.mean(acc, axis=-1, keepdims=True)
        xc = acc - mu
        y = xc * lax.rsqrt(jnp.mean(xc * xc, axis=-1, keepdims=True) + EPS)
        y = y * lng_ref[...] + lnb_ref[...]
        z_ref[base:base + LN_ROWS, :] = _silu(y).astype(bf16)

    out = _dot(z_ref[...], wout_ref[...]) + bout_ref[...]
    o_ref[...] = x + _rms(out, gpost_ref[...])


def _conv_mixer(x, gpre, gpost, w_in, b_in, dw_w, dw_b, ln_g, ln_b, w_out, b_out, *, tm):
    b, l, d = x.shape
    row = lambda v: v.reshape(1, -1)
    kern = functools.partial(_conv_mixer_kernel, tm=tm, d=d)
    return pl.pallas_call(
        kern,
        grid=(b, l // tm),
        in_specs=[
            pl.BlockSpec((None, tm, d), lambda bi, i: (bi, i, 0)),
            _const_spec((1, d)), _const_spec((1, d)),
            _const_spec((d, 2 * d)), _const_spec((1, 2 * d)),
            _const_spec((CONV_WIDTH, d)), _const_spec((1, d)),
            _const_spec((1, d)), _const_spec((1, d)),
            _const_spec((d, d)), _const_spec((1, d)),
        ],
        out_specs=pl.BlockSpec((None, tm, d), lambda bi, i: (bi, i, 0)),
        out_shape=jax.ShapeDtypeStruct((b, l, d), f32),
        scratch_shapes=[pltpu.VMEM((d // LANES, ROW_PITCH * (tm + HALO), LANES), f32),
                        pltpu.VMEM((tm, d), f32), pltpu.VMEM((tm, d), bf16)],
        compiler_params=pltpu.CompilerParams(
            dimension_semantics=("arbitrary", "arbitrary"), vmem_limit_bytes=VMEM_LIMIT),
        name="conv_mixer",
    )(x, row(gpre), row(gpost), w_in, row(b_in), dw_w, row(dw_b), row(ln_g), row(ln_b),
      w_out, row(b_out))


def _ffn_kernel(h_ref, gpre_ref, gpost_ref, wgu_ref, wd_ref, o_ref, a_ref, *, f, fc):
    x = h_ref[...]
    u = _rms(x, gpre_ref[...]).astype(bf16)
    for c in range(f // fc):
        gate = _dot(u, wgu_ref[:, c * fc:(c + 1) * fc])
        up = _dot(u, wgu_ref[:, f + c * fc:f + (c + 1) * fc])
        a_ref[:, c * fc:(c + 1) * fc] = (_silu(gate) * up).astype(bf16)
    out = _dot(a_ref[...], wd_ref[...])
    o_ref[...] = x + _rms(out, gpost_ref[...])


def _ffn(h, gpre, gpost, w_gu, w_down, *, tm, fc):
    t, d = h.shape
    f = w_down.shape[0]
    kern = functools.partial(_ffn_kernel, f=f, fc=fc)
    return pl.pallas_call(
        kern,
        grid=(t // tm,),
        in_specs=[
            pl.BlockSpec((tm, d), lambda i: (i, 0)),
            _const_spec((1, d)), _const_spec((1, d)),
            _const_spec((d, 2 * f)), _const_spec((f, d)),
        ],
        out_specs=pl.BlockSpec((tm, d), lambda i: (i, 0)),
        out_shape=jax.ShapeDtypeStruct((t, d), f32),
        scratch_shapes=[pltpu.VMEM((tm, f), bf16)],
        compiler_params=pltpu.CompilerParams(
            dimension_semantics=("arbitrary",), vmem_limit_bytes=VMEM_LIMIT),
        name="dense_ffn",
    )(h, gpre.reshape(1, d), gpost.reshape(1, d), w_gu, w_down)


def _hgrn_kernel(h_ref, gpre_ref, gpost_ref, win_ref, lbp_ref, ng_ref, wout_ref, wgu_ref, wdn_ref,
                 o_ref, wgu_bf_ref, wdn_bf_ref,
                 q_ref, k_ref, v_ref, g_ref, og_ref, oh_ref, st_ref, *, tm, d, layer):
    i = pl.program_id(1)
    heads = d // HEAD
    nsub = CHUNK // SUB

    wgu_bf_ref[...] = wgu_ref[...].astype(bf16)
    wdn_bf_ref[...] = wdn_ref[...].astype(bf16)

    @pl.when(i == 0)
    def _():
        st_ref[...] = jnp.zeros(st_ref.shape, f32)

    x = h_ref[...]
    u = _rms(x, gpre_ref[...]).astype(bf16)

    lbp = lbp_ref[...]
    e = jnp.exp(lbp - jnp.max(lbp, axis=0, keepdims=True))
    sm = e / jnp.sum(e, axis=0, keepdims=True)
    lb = jnp.sum(sm[1:layer + 1, :], axis=0, keepdims=True)

    q_ref[...] = _silu(_dot(u, win_ref[:, 0:d]))
    forget = lb + (1.0 - lb) * jax.nn.sigmoid(_dot(u, win_ref[:, d:2 * d]))
    k_ref[...] = 1.0 - forget
    g_ref[...] = jnp.log(forget)
    v_ref[...] = _dot(u, win_ref[:, 2 * d:3 * d]).astype(bf16)
    og_ref[...] = _silu(_dot(u, win_ref[:, 3 * d:4 * d]))

    rr = lax.broadcasted_iota(jnp.int32, (CHUNK, CHUNK), 0)
    cc = lax.broadcasted_iota(jnp.int32, (CHUNK, CHUNK), 1)
    tri = (rr >= cc).astype(bf16)
    diag_mask = (rr >= cc) & ((rr // SUB) == (cc // SUB))
    ng = ng_ref[...]
    zeros_blk = jnp.zeros((SUB, d), f32)

    def block_rows(vals):
        return jnp.concatenate([jnp.broadcast_to(r, (SUB, d)) for r in vals], axis=0)

    def chunk_body(c, carry):
        r0 = pl.multiple_of(c * CHUNK, CHUNK)
        rows = pl.ds(r0, CHUNK)
        g_hi, g_lo = _split_bf16(g_ref[rows, :])
        big_g = _dot(tri, g_hi) + _dot(tri, g_lo)
        q = q_ref[rows, :]
        k = k_ref[rows, :]

        ends = [big_g[SUB * b + SUB - 1:SUB * b + SUB, :] for b in range(nsub)]
        prevs = [jnp.zeros((1, d), f32)] + ends[:-1]
        ref_prev = block_rows(prevs)
        ref_end = block_rows(ends)
        qw = q * jnp.exp(big_g - ref_prev)
        k_end = k * jnp.exp(ref_end - big_g)
        qe = (qw * block_rows([jnp.exp(p) for p in prevs])).astype(bf16)
        k_dec = (k_end * block_rows([jnp.exp(ends[-1] - e) for e in ends])).astype(bf16)
        half = 0.5 * (ref_prev - ref_end)
        q_mid = (q * jnp.exp(jnp.clip(big_g - ref_prev + half, -EXP_CLAMP, EXP_CLAMP))).astype(bf16)
        k_mid = (k * jnp.exp(jnp.clip(ref_end - big_g + half, -EXP_CLAMP, EXP_CLAMP))).astype(bf16)
        q_parts, k_parts = [], []
        for jb in range(nsub - 1):
            qp = [zeros_blk] * (jb + 1)
            for b in range(jb + 1, nsub):
                piece = qw[SUB * b:SUB * (b + 1), :]
                if b > jb + 1:
                    piece = piece * jnp.exp(prevs[b] - ends[jb])
                qp.append(piece)
            q_parts.append(jnp.concatenate(qp, axis=0).astype(bf16))
            kp = [zeros_blk] * nsub
            kp[jb] = k_end[SUB * jb:SUB * (jb + 1), :]
            k_parts.append(jnp.concatenate(kp, axis=0).astype(bf16))
        decay = jnp.exp(ends[-1])

        for hd in range(heads):
            sl = slice(hd * HEAD, (hd + 1) * HEAD)
            v = v_ref[rows, sl]
            st = st_ref[hd]
            s_off = _dot_nt(jnp.concatenate([p[:, sl] for p in q_parts], axis=1),
                            jnp.concatenate([p[:, sl] for p in k_parts], axis=1))
            s_diag = _dot_nt(q_mid[:, sl], k_mid[:, sl])
            scores = (s_off + jnp.where(diag_mask, s_diag, 0.0)).astype(bf16)
            o = _dot_nt(qe[:, sl], st.astype(bf16)) + _dot(scores, v)
            st_ref[hd] = decay[:, sl] * st + _dot_tn(v, k_dec[:, sl])
            o = o * lax.rsqrt(jnp.mean(o * o, axis=-1, keepdims=True) + EPS) * ng
            oh_ref[rows, sl] = (o * og_ref[rows, sl]).astype(bf16)
        return carry

    lax.fori_loop(0, tm // CHUNK, chunk_body, 0, unroll=True)
    out = _dot(oh_ref[...], wout_ref[...])
    o_ref[...] = x + _rms(out, gpost_ref[...])


def _hgrn_mixer(h, gpre, gpost, w_in, lower_bounds, norm_g, w_out, moe_w_gu, moe_w_down,
                *, tm, layer):
    b, l, d = h.shape
    depth = lower_bounds.shape[0]
    n_tiles = l // tm
    steps = b * n_tiles
    wgu = moe_w_gu.reshape(-1, moe_w_gu.shape[-1])
    wdn = moe_w_down.reshape(-1, moe_w_down.shape[-1])
    gu_rows, dn_rows = wgu.shape[0] // steps, wdn.shape[0] // steps
    assert gu_rows * steps == wgu.shape[0] and dn_rows * steps == wdn.shape[0]
    slice_map = lambda bi, i: (bi * n_tiles + i, 0)
    kern = functools.partial(_hgrn_kernel, tm=tm, d=d, layer=layer)
    out, wgu_bf, wdn_bf = pl.pallas_call(
        kern,
        grid=(b, n_tiles),
        in_specs=[
            pl.BlockSpec((None, tm, d), lambda bi, i: (bi, i, 0)),
            _const_spec((1, d)), _const_spec((1, d)),
            _const_spec((d, 4 * d)), _const_spec((depth, d)), _const_spec((1, HEAD)),
            _const_spec((d, d)),
            pl.BlockSpec((gu_rows, wgu.shape[1]), slice_map),
            pl.BlockSpec((dn_rows, wdn.shape[1]), slice_map),
        ],
        out_specs=[
            pl.BlockSpec((None, tm, d), lambda bi, i: (bi, i, 0)),
            pl.BlockSpec((gu_rows, wgu.shape[1]), slice_map),
            pl.BlockSpec((dn_rows, wdn.shape[1]), slice_map),
        ],
        out_shape=[
            jax.ShapeDtypeStruct((b, l, d), f32),
            jax.ShapeDtypeStruct(wgu.shape, bf16),
            jax.ShapeDtypeStruct(wdn.shape, bf16),
        ],
        scratch_shapes=[pltpu.VMEM((tm, d), f32), pltpu.VMEM((tm, d), f32),
                        pltpu.VMEM((tm, d), bf16), pltpu.VMEM((tm, d), f32),
                        pltpu.VMEM((tm, d), f32), pltpu.VMEM((tm, d), bf16),
                        pltpu.VMEM((d // HEAD, HEAD, HEAD), f32)],
        compiler_params=pltpu.CompilerParams(
            dimension_semantics=("arbitrary", "arbitrary"), vmem_limit_bytes=VMEM_LIMIT),
        name="hgrn_mixer",
    )(h, gpre.reshape(1, d), gpost.reshape(1, d), w_in, lower_bounds, norm_g.reshape(1, HEAD), w_out,
      wgu, wdn)
    return out, wgu_bf.reshape(moe_w_gu.shape), wdn_bf.reshape(moe_w_down.shape)


def _router_kernel(h_ref, gpre_ref, wr_ref, up_ref, pos_ref, gate_ref, cnt_ref, carry_ref,
                   *, tm, d, n_exp):
    i = pl.program_id(0)

    @pl.when(i == 0)
    def _():
        carry_ref[...] = jnp.zeros(carry_ref.shape, f32)

    u = _rms(h_ref[...], gpre_ref[...])
    _to_slabs(up_ref, u)

    u_hi, u_lo = _split_bf16(u)
    w_hi, w_lo = _split_bf16(wr_ref[...])
    logits = _dot(u_hi, w_hi) + (_dot(u_hi, w_lo) + _dot(u_lo, w_hi))
    lane = lax.broadcasted_iota(jnp.int32, (tm, LANES), 1)
    neg = jnp.float32(-jnp.inf)
    lg = jnp.where(lane < n_exp, logits, neg)
    v1 = jnp.max(lg, axis=-1, keepdims=True)
    i1 = jnp.min(jnp.where(lg == v1, lane, LANES), axis=-1, keepdims=True)
    m1 = lane == i1
    lg2 = jnp.where(m1, neg, lg)
    v2 = jnp.max(lg2, axis=-1, keepdims=True)
    i2 = jnp.min(jnp.where(lg2 == v2, lane, LANES), axis=-1, keepdims=True)
    m2 = lane == i2
    dd = jnp.exp(v2 - v1)
    w1 = 1.0 / (1.0 + dd)
    w2 = dd / (1.0 + dd)

    sel = (m1 | m2).astype(f32)
    rr = lax.broadcasted_iota(jnp.int32, (tm, tm), 0)
    cc = lax.broadcasted_iota(jnp.int32, (tm, tm), 1)
    cum = _dot((rr >= cc).astype(bf16), sel.astype(bf16))
    rank = cum - sel + carry_ref[...]
    r1 = jnp.sum(jnp.where(m1, rank, 0.0), axis=-1, keepdims=True).astype(jnp.int32)
    r2 = jnp.sum(jnp.where(m2, rank, 0.0), axis=-1, keepdims=True).astype(jnp.int32)
    pos_ref[...] = jnp.where(lane == 0, r1, jnp.where(lane == 1, r2, jnp.where(
        lane == 2, i1, jnp.where(lane == 3, i2, 0))))[:, :ROUTE_LANES]
    gate_ref[...] = jnp.where(lane == 0, w1, jnp.where(lane == 1, w2, 0.0))[:, :ROUTE_LANES]
    carry_ref[...] = carry_ref[...] + cum[tm - 1:tm, :]
    cnt_ref[...] = carry_ref[...].astype(jnp.int32)


def _router(h, gpre, w_router, *, tm):
    t, d = h.shape
    n_exp = w_router.shape[1]
    wr = jnp.zeros((d, LANES), f32).at[:, :n_exp].set(w_router)
    kern = functools.partial(_router_kernel, tm=tm, d=d, n_exp=n_exp)
    return pl.pallas_call(
        kern,
        grid=(t // tm,),
        in_specs=[
            pl.BlockSpec((tm, d), lambda i: (i, 0)),
            _const_spec((1, d)), _const_spec((d, LANES)),
        ],
        out_specs=[
            pl.BlockSpec((tm * (d // LANES), LANES), lambda i: (i, 0)),
            pl.BlockSpec((tm, ROUTE_LANES), lambda i: (i, 0)),
            pl.BlockSpec((tm, ROUTE_LANES), lambda i: (i, 0)),
            pl.BlockSpec((1, LANES), lambda i: (0, 0)),
        ],
        out_shape=[
            jax.ShapeDtypeStruct((t * (d // LANES), LANES), f32),
            jax.ShapeDtypeStruct((t, ROUTE_LANES), jnp.int32),
            jax.ShapeDtypeStruct((t, ROUTE_LANES), f32),
            jax.ShapeDtypeStruct((1, LANES), jnp.int32),
        ],
        scratch_shapes=[pltpu.VMEM((1, LANES), f32)],
        compiler_params=pltpu.CompilerParams(
            dimension_semantics=("arbitrary",), vmem_limit_bytes=VMEM_LIMIT),
        name="moe_router",
    )(h, gpre.reshape(1, d), wr)


def _dispatch_kernel(cnt_ref, start_ref, nv_ref, pos_ref, up_ref, xs_ref, tile_ref, zero_ref,
                     fsem, sem, zsem, *, tm, te, n_exp, nt, s_n):
    i = pl.program_id(0)
    last = pl.num_programs(0) - 1
    slot = i % RING

    def slab(r, n=1):
        return pl.ds(pl.multiple_of(r * s_n, s_n), n * s_n)

    def fetch(step, sl):
        return pltpu.make_async_copy(up_ref.at[slab(step * tm, tm)], tile_ref.at[sl], fsem.at[sl])

    def drain_step(sl):
        for _ in range(TOP_K):
            pltpu.make_async_copy(tile_ref.at[sl], xs_ref.at[slab(0, tm)], sem.at[sl]).wait()

    @pl.when(i == 0)
    def _():
        fetch(0, 0).start()

    @pl.when(i < last)
    def _():
        fetch(i + 1, (i + 1) % RING).start()

    fetch(i, slot).wait()

    def issue(r, carry):
        for kk in range(TOP_K):
            pltpu.make_async_copy(tile_ref.at[slot, slab(r)],
                                  xs_ref.at[slab(pos_ref[TOP_K * r + kk])],
                                  sem.at[slot]).start(priority=kk)
        return carry

    lax.fori_loop(0, tm, issue, 0, unroll=8)

    @pl.when(i > 0)
    def _():
        drain_step((i + RING - 1) % RING)

    @pl.when(i == last)
    def _():
        drain_step(slot)
        zero_ref[...] = jnp.zeros(zero_ref.shape, zero_ref.dtype)

        def zero_row(dst):
            return pltpu.make_async_copy(zero_ref.at[slab(0)], xs_ref.at[slab(dst)], zsem)

        def zero_tile(ti):
            return pltpu.make_async_copy(zero_ref, xs_ref.at[slab(ti * te, te)], zsem)

        for ex in range(n_exp):
            lo = start_ref[ex] + cnt_ref[ex]
            hi = start_ref[ex] + ((cnt_ref[ex] + te - 1) // te) * te
            lax.fori_loop(lo, hi, lambda r, c: (zero_row(r).start(), c)[1], 0)
            lax.fori_loop(lo, hi, lambda r, c: (zero_row(r).wait(), c)[1], 0)
        lax.fori_loop(nv_ref[0], nt, lambda ti, c: (zero_tile(ti).start(), c)[1], 0)
        lax.fori_loop(nv_ref[0], nt, lambda ti, c: (zero_tile(ti).wait(), c)[1], 0)


def _dispatch(counts, starts, nvalid, pos_flat, upk, *, tm, te, n_exp, nt, s_n):
    t = upk.shape[0] // s_n
    kern = functools.partial(_dispatch_kernel, tm=tm, te=te, n_exp=n_exp, nt=nt, s_n=s_n)
    grid_spec = pltpu.PrefetchScalarGridSpec(
        num_scalar_prefetch=3,
        grid=(t // tm,),
        in_specs=[
            pl.BlockSpec((TOP_K * tm,), lambda i, *_: (i,), memory_space=pltpu.SMEM),
            pl.BlockSpec(memory_space=pl.ANY),
        ],
        out_specs=pl.BlockSpec(memory_space=pl.ANY),
        scratch_shapes=[pltpu.VMEM((RING, tm * s_n, LANES), upk.dtype),
                        pltpu.VMEM((te * s_n, LANES), upk.dtype),
                        pltpu.SemaphoreType.DMA((RING,)), pltpu.SemaphoreType.DMA((RING,)),
                        pltpu.SemaphoreType.DMA],
    )
    return pl.pallas_call(
        kern,
        grid_spec=grid_spec,
        out_shape=jax.ShapeDtypeStruct((nt * te * s_n, LANES), upk.dtype),
        compiler_params=pltpu.CompilerParams(
            dimension_semantics=("arbitrary",), vmem_limit_bytes=VMEM_LIMIT),
        name="moe_dispatch",
    )(counts, starts, nvalid, pos_flat, upk)


def _expert_kernel(te_ref, nv_ref, xs_ref, wg_ref, wu_ref, wd_ref, y_ref, a_ref, acc_ref,
                   *, fc, sc):
    i = pl.program_id(0)
    j = pl.program_id(1)

    @pl.when((i == 0) & (j == 0))
    def _():
        acc_ref[...] = jnp.zeros(acc_ref.shape, f32)

    @pl.when((i >= nv_ref[0]) & (j == 0))
    def _():
        y_ref[...] = jnp.zeros(y_ref.shape, f32)

    @pl.when(i < nv_ref[0])
    def _():
        x = _from_slabs(xs_ref, *acc_ref.shape).astype(bf16)
        for c in range(fc // sc):
            gate = _dot(x, wg_ref[:, c * sc:(c + 1) * sc])
            up = _dot(x, wu_ref[:, c * sc:(c + 1) * sc])
            a_ref[:, c * sc:(c + 1) * sc] = (_silu(gate) * up).astype(bf16)
        part = _dot(a_ref[...], wd_ref[...])
        acc = jnp.where(j == 0, part, acc_ref[...] + part)
        acc_ref[...] = acc
        _to_slabs(y_ref, acc)


def _experts(tile_e, nvalid, xs, w_gu, w_down, *, te, fc, sc):
    n_exp, d, f2 = w_gu.shape
    s_n = d // LANES
    f = f2 // 2
    nf = f // fc
    nt = tile_e.shape[0]

    def jj(i, j, nv):
        return jnp.where(i < nv[0], j, nf - 1)

    grid_spec = pltpu.PrefetchScalarGridSpec(
        num_scalar_prefetch=2,
        grid=(nt, nf),
        in_specs=[
            pl.BlockSpec((te * s_n, LANES), lambda i, j, e, nv: (i, 0)),
            pl.BlockSpec((None, d, fc), lambda i, j, e, nv: (e[i], 0, jj(i, j, nv))),
            pl.BlockSpec((None, d, fc), lambda i, j, e, nv: (e[i], 0, nf + jj(i, j, nv))),
            pl.BlockSpec((None, fc, d), lambda i, j, e, nv: (e[i], jj(i, j, nv), 0)),
        ],
        out_specs=pl.BlockSpec((te * s_n, LANES), lambda i, j, e, nv: (i, 0)),
        scratch_shapes=[pltpu.VMEM((te, fc), bf16), pltpu.VMEM((te, d), f32)],
    )
    kern = functools.partial(_expert_kernel, fc=fc, sc=sc)
    return pl.pallas_call(
        kern,
        grid_spec=grid_spec,
        out_shape=jax.ShapeDtypeStruct(xs.shape, f32),
        compiler_params=pltpu.CompilerParams(
            dimension_semantics=("arbitrary", "arbitrary"), vmem_limit_bytes=VMEM_LIMIT),
        name="moe_experts",
    )(tile_e, nvalid, xs, w_gu, w_gu, w_down)


def _combine_kernel(pos_ref, posn_ref, y_ref, gate_ref, h_ref, gpost_ref, o_ref, buf_ref, sem,
                    *, tm, s_n):
    i = pl.program_id(0)
    slot = i % 2

    def slab(r, n=1):
        return pl.ds(pl.multiple_of(r * s_n, s_n), n * s_n)

    def issue(p_ref, sl):
        def body(r, carry):
            for kk in range(TOP_K):
                pltpu.make_async_copy(y_ref.at[slab(p_ref[TOP_K * r + kk])],
                                      buf_ref.at[sl, kk, slab(r)], sem.at[sl]).start()
            return carry
        lax.fori_loop(0, tm, body, 0, unroll=8)

    @pl.when(i == 0)
    def _():
        issue(pos_ref, 0)

    @pl.when(i + 1 < pl.num_programs(0))
    def _():
        issue(posn_ref, 1 - slot)

    for kk in range(TOP_K):
        pltpu.make_async_copy(y_ref.at[slab(0, tm)], buf_ref.at[slot, kk], sem.at[slot]).wait()
    gates = gate_ref[...]
    mix = (gates[:, 0:1] * _from_slabs(buf_ref.at[slot, 0], *h_ref.shape)
           + gates[:, 1:2] * _from_slabs(buf_ref.at[slot, 1], *h_ref.shape))
    o_ref[...] = h_ref[...] + _rms(mix, gpost_ref[...])


def _combine(pos_flat, y, gates, h, gpost, *, tm):
    t, d = h.shape
    s_n = d // LANES
    kern = functools.partial(_combine_kernel, tm=tm, s_n=s_n)
    n_steps = t // tm
    return pl.pallas_call(
        kern,
        grid=(n_steps,),
        in_specs=[
            pl.BlockSpec((TOP_K * tm,), lambda i: (i,), memory_space=pltpu.SMEM),
            pl.BlockSpec((TOP_K * tm,), lambda i: (jnp.minimum(i + 1, n_steps - 1),),
                         memory_space=pltpu.SMEM),
            pl.BlockSpec(memory_space=pl.ANY),
            pl.BlockSpec((tm, ROUTE_LANES), lambda i: (i, 0)),
            pl.BlockSpec((tm, d), lambda i: (i, 0)),
            _const_spec((1, d)),
        ],
        out_specs=pl.BlockSpec((tm, d), lambda i: (i, 0)),
        out_shape=jax.ShapeDtypeStruct((t, d), f32),
        scratch_shapes=[pltpu.VMEM((2, TOP_K, tm * s_n, LANES), f32),
                        pltpu.SemaphoreType.DMA((2,))],
        compiler_params=pltpu.CompilerParams(
            dimension_semantics=("arbitrary",), vmem_limit_bytes=VMEM_LIMIT),
        name="moe_combine",
    )(pos_flat, pos_flat, y, gates, h, gpost.reshape(1, d))


def _moe(h, gpre, gpost, w_router, w_gu, w_down, *, tm_route, tm_move, te, fc, sc):
    t, d = h.shape
    n_exp = w_router.shape[1]
    upk, pos, gates, counts = _router(h, gpre, w_router, tm=tm_route)
    counts = counts[0, :n_exp]

    nt = (TOP_K * t) // te + n_exp
    tiles_per = (counts + te - 1) // te
    tile_start = jnp.cumsum(tiles_per) - tiles_per
    nvalid = jnp.sum(tiles_per).reshape(1).astype(jnp.int32)
    idx = jnp.minimum(jnp.arange(nt, dtype=jnp.int32), jnp.maximum(nvalid - 1, 0))
    tile_e = (jnp.sum(idx[:, None] >= tile_start[None, :], axis=1) - 1).astype(jnp.int32)
    row_start = (tile_start * te).astype(jnp.int32)
    pos_flat = (pos[:, :TOP_K] + row_start[pos[:, TOP_K:2 * TOP_K]]).reshape(-1)

    xs = _dispatch(counts, row_start, nvalid, pos_flat, upk, tm=tm_move, te=te, n_exp=n_exp, nt=nt,
                   s_n=d // LANES)
    y = _experts(tile_e, nvalid, xs, w_gu, w_down, te=te, fc=fc, sc=sc)
    return _combine(pos_flat, y, gates, h, gpost, tm=tm_move)


def _pick(n, pref):
    return pref if n % pref == 0 else n


def kernel(x, norm_g, conv_w_in, conv_b_in, conv_dw_w, conv_dw_b, conv_ln_g, conv_ln_b, conv_w_out, conv_b_out, hgrn_w_in, hgrn_lower_bounds, hgrn_norm_g, hgrn_w_out, ffn_w_gu, ffn_w_down, moe_router, moe_w_gu, moe_w_down):
    b, l, d = x.shape
    depth = norm_g.shape[0]
    t = b * l
    tm_seq = _pick(l, 512)
    tm_tok = _pick(t, 512)
    h = x
    for i in range(depth):
        j = i // 2
        if i % 2 == 0:
            h = _conv_mixer(h, norm_g[i, 0], norm_g[i, 1], conv_w_in[j].astype(bf16), conv_b_in[j],
                            conv_dw_w[j], conv_dw_b[j], conv_ln_g[j], conv_ln_b[j],
                            conv_w_out[j].astype(bf16), conv_b_out[j], tm=_pick(l, 1024))
            f = ffn_w_down.shape[1]
            h = _ffn(h.reshape(t, d), norm_g[i, 2], norm_g[i, 3], ffn_w_gu[j].astype(bf16),
                     ffn_w_down[j].astype(bf16), tm=_pick(t, 1024), fc=_pick(f, 256)).reshape(b, l, d)
        else:
            h, w_gu_bf, w_down_bf = _hgrn_mixer(
                h, norm_g[i, 0], norm_g[i, 1], hgrn_w_in[j].astype(bf16), hgrn_lower_bounds,
                hgrn_norm_g[j], hgrn_w_out[j].astype(bf16), moe_w_gu[j], moe_w_down[j],
                tm=tm_seq, layer=i)
            f = moe_w_down.shape[2]
            fc = f // 2 if (f // 2) % LANES == 0 else f
            h = _moe(h.reshape(t, d), norm_g[i, 2], norm_g[i, 3], moe_router[j],
                     w_gu_bf, w_down_bf,
                     tm_route=tm_tok, tm_move=tm_tok, te=_pick(t, 512), fc=fc,
                     sc=_pick(fc, 256)).reshape(b, l, d)
    return h
```

```python
import functools

import jax
import jax.numpy as jnp
from jax import lax
from jax.experimental import pallas as pl
from jax.experimental.pallas import tpu as pltpu

EPS = 1e-6
CHUNK = 64
SUB = 16
HEAD = 128
CONV_WIDTH = 31
HALO = 32
CONV_ROWS = 64
LN_ROWS = 32
ROW_PITCH = 2
LANES = 128
SUBLANES = 8
TOP_K = 2
ROUTE_LANES = 8
RING = 3
EXP_CLAMP = 80.0
VMEM_LIMIT = 56 * 1024 * 1024

f32 = jnp.float32
bf16 = jnp.bfloat16


def _dot(a, b):
    return jnp.dot(a, b, preferred_element_type=f32)


def _dot_nt(a, b):
    return lax.dot_general(a, b, (((1,), (1,)), ((), ())), preferred_element_type=f32)


def _dot_tn(a, b):
    return lax.dot_general(a, b, (((0,), (0,)), ((), ())), preferred_element_type=f32)


def _rms(x, g):
    return x * lax.rsqrt(jnp.mean(x * x, axis=-1, keepdims=True) + EPS) * g


def _silu(x):
    return x * jax.nn.sigmoid(x)


def _to_slabs(ref, x):
    n, d = x.shape
    s_n = d // LANES
    for s in range(s_n):
        ref[pl.ds(s, n, stride=s_n), :] = x[:, s * LANES:(s + 1) * LANES]


def _from_slabs(ref, n, d):
    s_n = d // LANES
    return jnp.concatenate([ref[pl.ds(s, n, stride=s_n), :] for s in range(s_n)], axis=1)


def _split_bf16(x):
    hi = x.astype(bf16)
    return hi, (x - hi.astype(f32)).astype(bf16)


def _const_spec(shape):
    nd = len(shape)
    return pl.BlockSpec(shape, lambda *_: (0,) * nd, pipeline_mode=pl.Buffered(1))


def _conv_mixer_kernel(x_ref, gpre_ref, gpost_ref, win_ref, bin_ref, dww_ref, dwb_ref,
                       lng_ref, lnb_ref, wout_ref, bout_ref, o_ref, gl_ref, y_ref, z_ref,
                       *, tm, d):
    i = pl.program_id(1)
    nslab = d // LANES

    def rows(start, n):
        return pl.ds(ROW_PITCH * start, n, stride=ROW_PITCH)

    @pl.when(i == 0)
    def _():
        for c in range(nslab):
            gl_ref[c, rows(0, HALO), :] = jnp.zeros((HALO, LANES), f32)

    @pl.when(i > 0)
    def _():
        for c in range(nslab):
            gl_ref[c, rows(0, HALO), :] = gl_ref[c, rows(tm, HALO), :]

    x = x_ref[...]
    u = _rms(x, gpre_ref[...]).astype(bf16)
    a = _dot(u, win_ref[:, :d]) + bin_ref[:, :d]
    gate = _dot(u, win_ref[:, d:]) + bin_ref[:, d:]
    glu = a * jax.nn.sigmoid(gate)
    for c in range(nslab):
        gl_ref[c, rows(HALO, tm), :] = glu[:, c * LANES:(c + 1) * LANES]

    off = HALO - (CONV_WIDTH - 1)
    for c in range(nslab):
        cols = slice(c * LANES, (c + 1) * LANES)
        for rb in range(tm // CONV_ROWS):
            base = rb * CONV_ROWS
            acc = jnp.broadcast_to(dwb_ref[:, cols], (CONV_ROWS, LANES))
            for j in range(CONV_WIDTH):
                acc = acc + dww_ref[j:j + 1, cols] * gl_ref[c, rows(base + off + j, CONV_ROWS), :]
            y_ref[base:base + CONV_ROWS, cols] = acc

    for rb in range(tm // LN_ROWS):
        base = rb * LN_ROWS
        acc = y_ref[base:base + LN_ROWS, :]
        mu = jnp.mean(acc, axis=-1, keepdims=True)
        xc = acc - mu
        y = xc * lax.rsqrt(jnp.mean(xc * xc, axis=-1, keepdims=True) + EPS)
        y = y * lng_ref[...] + lnb_ref[...]
        z_ref[base:base + LN_ROWS, :] = _silu(y).astype(bf16)

    out = _dot(z_ref[...], wout_ref[...]) + bout_ref[...]
    o_ref[...] = x + _rms(out, gpost_ref[...])


def _conv_mixer(x, gpre, gpost, w_in, b_in, dw_w, dw_b, ln_g, ln_b, w_out, b_out, *, tm):
    b, l, d = x.shape
    row = lambda v: v.reshape(1, -1)
    kern = functools.partial(_conv_mixer_kernel, tm=tm, d=d)
    return pl.pallas_call(
        kern,
        grid=(b, l // tm),
        in_specs=[
            pl.BlockSpec((None, tm, d), lambda bi, i: (bi, i, 0)),
            _const_spec((1, d)), _const_spec((1, d)),
            _const_spec((d, 2 * d)), _const_spec((1, 2 * d)),
            _const_spec((CONV_WIDTH, d)), _const_spec((1, d)),
            _const_spec((1, d)), _const_spec((1, d)),
            _const_spec((d, d)), _const_spec((1, d)),
        ],
        out_specs=pl.BlockSpec((None, tm, d), lambda bi, i: (bi, i, 0)),
        out_shape=jax.ShapeDtypeStruct((b, l, d), f32),
        scratch_shapes=[pltpu.VMEM((d // LANES, ROW_PITCH * (tm + HALO), LANES), f32),
                        pltpu.VMEM((tm, d), f32), pltpu.VMEM((tm, d), bf16)],
        compiler_params=pltpu.CompilerParams(
            dimension_semantics=("arbitrary", "arbitrary"), vmem_limit_bytes=VMEM_LIMIT),
        name="conv_mixer",
    )(x, row(gpre), row(gpost), w_in, row(b_in), dw_w, row(dw_b), row(ln_g), row(ln_b),
      w_out, row(b_out))


def _ffn_kernel(h_ref, gpre_ref, gpost_ref, wgu_ref, wd_ref, o_ref, a_ref, *, f, fc):
    x = h_ref[...]
    u = _rms(x, gpre_ref[...]).astype(bf16)
    for c in range(f // fc):
        gate = _dot(u, wgu_ref[:, c * fc:(c + 1) * fc])
        up = _dot(u, wgu_ref[:, f + c * fc:f + (c + 1) * fc])
        a_ref[:, c * fc:(c + 1) * fc] = (_silu(gate) * up).astype(bf16)
    out = _dot(a_ref[...], wd_ref[...])
    o_ref[...] = x + _rms(out, gpost_ref[...])


def _ffn(h, gpre, gpost, w_gu, w_down, *, tm, fc):
    t, d = h.shape
    f = w_down.shape[0]
    kern = functools.partial(_ffn_kernel, f=f, fc=fc)
    return pl.pallas_call(
        kern,
        grid=(t // tm,),
        in_specs=[
            pl.BlockSpec((tm, d), lambda i: (i, 0)),
            _const_spec((1, d)), _const_spec((1, d)),
            _const_spec((d, 2 * f)), _const_spec((f, d)),
        ],
        out_specs=pl.BlockSpec((tm, d), lambda i: (i, 0)),
        out_shape=jax.ShapeDtypeStruct((t, d), f32),
        scratch_shapes=[pltpu.VMEM((tm, f), bf16)],
        compiler_params=pltpu.CompilerParams(
            dimension_semantics=("arbitrary",), vmem_limit_bytes=VMEM_LIMIT),
        name="dense_ffn",
    )(h, gpre.reshape(1, d), gpost.reshape(1, d), w_gu, w_down)


def _hgrn_kernel(h_ref, gpre_ref, gpost_ref, win_ref, lbp_ref, ng_ref, wout_ref, wgu_ref, wdn_ref,
                 o_ref, wgu_bf_ref, wdn_bf_ref,
                 q_ref, k_ref, v_ref, g_ref, og_ref, oh_ref, st_ref, *, tm, d, layer):
    i = pl.program_id(1)
    heads = d // HEAD
    nsub = CHUNK // SUB

    wgu_bf_ref[...] = wgu_ref[...].astype(bf16)
    wdn_bf_ref[...] = wdn_ref[...].astype(bf16)

    @pl.when(i == 0)
    def _():
        st_ref[...] = jnp.zeros(st_ref.shape, f32)

    x = h_ref[...]
    u = _rms(x, gpre_ref[...]).astype(bf16)

    lbp = lbp_ref[...]
    e = jnp.exp(lbp - jnp.max(lbp, axis=0, keepdims=True))
    sm = e / jnp.sum(e, axis=0, keepdims=True)
    lb = jnp.sum(sm[1:layer + 1, :], axis=0, keepdims=True)

    q_ref[...] = _silu(_dot(u, win_ref[:, 0:d]))
    forget = lb + (1.0 - lb) * jax.nn.sigmoid(_dot(u, win_ref[:, d:2 * d]))
    k_ref[...] = 1.0 - forget
    g_ref[...] = jnp.log(forget)
    v_ref[...] = _dot(u, win_ref[:, 2 * d:3 * d]).astype(bf16)
    og_ref[...] = _silu(_dot(u, win_ref[:, 3 * d:4 * d]))

    rr = lax.broadcasted_iota(jnp.int32, (CHUNK, CHUNK), 0)
    cc = lax.broadcasted_iota(jnp.int32, (CHUNK, CHUNK), 1)
    tri = (rr >= cc).astype(bf16)
    diag_mask = (rr >= cc) & ((rr // SUB) == (cc // SUB))
    ng = ng_ref[...]
    zeros_blk = jnp.zeros((SUB, d), f32)

    def block_rows(vals):
        return jnp.concatenate([jnp.broadcast_to(r, (SUB, d)) for r in vals], axis=0)

    def chunk_body(c, carry):
        r0 = pl.multiple_of(c * CHUNK, CHUNK)
        rows = pl.ds(r0, CHUNK)
        g_hi, g_lo = _split_bf16(g_ref[rows, :])
        big_g = _dot(tri, g_hi) + _dot(tri, g_lo)
        q = q_ref[rows, :]
        k = k_ref[rows, :]

        ends = [big_g[SUB * b + SUB - 1:SUB * b + SUB, :] for b in range(nsub)]
        prevs = [jnp.zeros((1, d), f32)] + ends[:-1]
        ref_prev = block_rows(prevs)
        ref_end = block_rows(ends)
        qw = q * jnp.exp(big_g - ref_prev)
        k_end = k * jnp.exp(ref_end - big_g)
        qe = (qw * block_rows([jnp.exp(p) for p in prevs])).astype(bf16)
        k_dec = (k_end * block_rows([jnp.exp(ends[-1] - e) for e in ends])).astype(bf16)
        half = 0.5 * (ref_prev - ref_end)
        q_mid = (q * jnp.exp(jnp.clip(big_g - ref_prev + half, -EXP_CLAMP, EXP_CLAMP))).astype(bf16)
        k_mid = (k * jnp.exp(jnp.clip(ref_end - big_g + half, -EXP_CLAMP, EXP_CLAMP))).astype(bf16)
        q_parts, k_parts = [], []
        for jb in range(nsub - 1):
            qp = [zeros_blk] * (jb + 1)
            for b in range(jb + 1, nsub):
                piece = qw[SUB * b:SUB * (b + 1), :]
                if b > jb + 1:
                    piece = piece * jnp.exp(prevs[b] - ends[jb])
                qp.append(piece)
            q_parts.append(jnp.concatenate(qp, axis=0).astype(bf16))
            kp = [zeros_blk] * nsub
            kp[jb] = k_end[SUB * jb:SUB * (jb + 1), :]
            k_parts.append(jnp.concatenate(kp, axis=0).astype(bf16))
        decay = jnp.exp(ends[-1])

        for hd in range(heads):
            sl = slice(hd * HEAD, (hd + 1) * HEAD)
            v = v_ref[rows, sl]
            st = st_ref[hd]
            s_off = _dot_nt(jnp.concatenate([p[:, sl] for p in q_parts], axis=1),
                            jnp.concatenate([p[:, sl] for p in k_parts], axis=1))
            s_diag = _dot_nt(q_mid[:, sl], k_mid[:, sl])
            scores = (s_off + jnp.where(diag_mask, s_diag, 0.0)).astype(bf16)
            o = _dot_nt(qe[:, sl], st.astype(bf16)) + _dot(scores, v)
            st_ref[hd] = decay[:, sl] * st + _dot_tn(v, k_dec[:, sl])
            o = o * lax.rsqrt(jnp.mean(o * o, axis=-1, keepdims=True) + EPS) * ng
            oh_ref[rows, sl] = (o * og_ref[rows, sl]).astype(bf16)
        return carry

    lax.fori_loop(0, tm // CHUNK, chunk_body, 0, unroll=True)
    out = _dot(oh_ref[...], wout_ref[...])
    o_ref[...] = x + _rms(out, gpost_ref[...])


def _hgrn_mixer(h, gpre, gpost, w_in, lower_bounds, norm_g, w_out, moe_w_gu, moe_w_down,
                *, tm, layer):
    b, l, d = h.shape
    depth = lower_bounds.shape[0]
    n_tiles = l // tm
    steps = b * n_tiles
    wgu = moe_w_gu.reshape(-1, moe_w_gu.shape[-1])
    wdn = moe_w_down.reshape(-1, moe_w_down.shape[-1])
    gu_rows, dn_rows = wgu.shape[0] // steps, wdn.shape[0] // steps
    assert gu_rows * steps == wgu.shape[0] and dn_rows * steps == wdn.shape[0]
    slice_map = lambda bi, i: (bi * n_tiles + i, 0)
    kern = functools.partial(_hgrn_kernel, tm=tm, d=d, layer=layer)
    out, wgu_bf, wdn_bf = pl.pallas_call(
        kern,
        grid=(b, n_tiles),
        in_specs=[
            pl.BlockSpec((None, tm, d), lambda bi, i: (bi, i, 0)),
            _const_spec((1, d)), _const_spec((1, d)),
            _const_spec((d, 4 * d)), _const_spec((depth, d)), _const_spec((1, HEAD)),
            _const_spec((d, d)),
            pl.BlockSpec((gu_rows, wgu.shape[1]), slice_map),
            pl.BlockSpec((dn_rows, wdn.shape[1]), slice_map),
        ],
        out_specs=[
            pl.BlockSpec((None, tm, d), lambda bi, i: (bi, i, 0)),
            pl.BlockSpec((gu_rows, wgu.shape[1]), slice_map),
            pl.BlockSpec((dn_rows, wdn.shape[1]), slice_map),
        ],
        out_shape=[
            jax.ShapeDtypeStruct((b, l, d), f32),
            jax.ShapeDtypeStruct(wgu.shape, bf16),
            jax.ShapeDtypeStruct(wdn.shape, bf16),
        ],
        scratch_shapes=[pltpu.VMEM((tm, d), f32), pltpu.VMEM((tm, d), f32),
                        pltpu.VMEM((tm, d), bf16), pltpu.VMEM((tm, d), f32),
                        pltpu.VMEM((tm, d), f32), pltpu.VMEM((tm, d), bf16),
                        pltpu.VMEM((d // HEAD, HEAD, HEAD), f32)],
        compiler_params=pltpu.CompilerParams(
            dimension_semantics=("arbitrary", "arbitrary"), vmem_limit_bytes=VMEM_LIMIT),
        name="hgrn_mixer",
    )(h, gpre.reshape(1, d), gpost.reshape(1, d), w_in, lower_bounds, norm_g.reshape(1, HEAD), w_out,
      wgu, wdn)
    return out, wgu_bf.reshape(moe_w_gu.shape), wdn_bf.reshape(moe_w_down.shape)


def _router_kernel(h_ref, gpre_ref, wr_ref, up_ref, pos_ref, gate_ref, cnt_ref, carry_ref,
                   *, tm, d, n_exp):
    i = pl.program_id(0)

    @pl.when(i == 0)
    def _():
        carry_ref[...] = jnp.zeros(carry_ref.shape, f32)

    u = _rms(h_ref[...], gpre_ref[...])
    _to_slabs(up_ref, u)

    u_hi, u_lo = _split_bf16(u)
    w_hi, w_lo = _split_bf16(wr_ref[...])
    logits = _dot(u_hi, w_hi) + (_dot(u_hi, w_lo) + _dot(u_lo, w_hi))
    lane = lax.broadcasted_iota(jnp.int32, (tm, LANES), 1)
    neg = jnp.float32(-jnp.inf)
    lg = jnp.where(lane < n_exp, logits, neg)
    v1 = jnp.max(lg, axis=-1, keepdims=True)
    i1 = jnp.min(jnp.where(lg == v1, lane, LANES), axis=-1, keepdims=True)
    m1 = lane == i1
    lg2 = jnp.where(m1, neg, lg)
    v2 = jnp.max(lg2, axis=-1, keepdims=True)
    i2 = jnp.min(jnp.where(lg2 == v2, lane, LANES), axis=-1, keepdims=True)
    m2 = lane == i2
    dd = jnp.exp(v2 - v1)
    w1 = 1.0 / (1.0 + dd)
    w2 = dd / (1.0 + dd)

    sel = (m1 | m2).astype(f32)
    rr = lax.broadcasted_iota(jnp.int32, (tm, tm), 0)
    cc = lax.broadcasted_iota(jnp.int32, (tm, tm), 1)
    cum = _dot((rr >= cc).astype(bf16), sel.astype(bf16))
    rank = cum - sel + carry_ref[...]
    r1 = jnp.sum(jnp.where(m1, rank, 0.0), axis=-1, keepdims=True).astype(jnp.int32)
    r2 = jnp.sum(jnp.where(m2, rank, 0.0), axis=-1, keepdims=True).astype(jnp.int32)
    pos_ref[...] = jnp.where(lane == 0, r1, jnp.where(lane == 1, r2, jnp.where(
        lane == 2, i1, jnp.where(lane == 3, i2, 0))))[:, :ROUTE_LANES]
    gate_ref[...] = jnp.where(lane == 0, w1, jnp.where(lane == 1, w2, 0.0))[:, :ROUTE_LANES]
    carry_ref[...] = carry_ref[...] + cum[tm - 1:tm, :]
    cnt_ref[...] = carry_ref[...].astype(jnp.int32)


def _router(h, gpre, w_router, *, tm):
    t, d = h.shape
    n_exp = w_router.shape[1]
    wr = jnp.zeros((d, LANES), f32).at[:, :n_exp].set(w_router)
    kern = functools.partial(_router_kernel, tm=tm, d=d, n_exp=n_exp)
    return pl.pallas_call(
        kern,
        grid=(t // tm,),
        in_specs=[
            pl.BlockSpec((tm, d), lambda i: (i, 0)),
            _const_spec((1, d)), _const_spec((d, LANES)),
        ],
        out_specs=[
            pl.BlockSpec((tm * (d // LANES), LANES), lambda i: (i, 0)),
            pl.BlockSpec((tm, ROUTE_LANES), lambda i: (i, 0)),
            pl.BlockSpec((tm, ROUTE_LANES), lambda i: (i, 0)),
            pl.BlockSpec((1, LANES), lambda i: (0, 0)),
        ],
        out_shape=[
            jax.ShapeDtypeStruct((t * (d // LANES), LANES), f32),
            jax.ShapeDtypeStruct((t, ROUTE_LANES), jnp.int32),
            jax.ShapeDtypeStruct((t, ROUTE_LANES), f32),
            jax.ShapeDtypeStruct((1, LANES), jnp.int32),
        ],
        scratch_shapes=[pltpu.VMEM((1, LANES), f32)],
        compiler_params=pltpu.CompilerParams(
            dimension_semantics=("arbitrary",), vmem_limit_bytes=VMEM_LIMIT),
        name="moe_router",
    )(h, gpre.reshape(1, d), wr)


def _dispatch_kernel(cnt_ref, start_ref, nv_ref, pos_ref, up_ref, xs_ref, tile_ref, zero_ref,
                     fsem, sem, zsem, *, tm, te, n_exp, nt, s_n):
    i = pl.program_id(0)
    last = pl.num_programs(0) - 1
    slot = i % RING

    def slab(r, n=1):
        return pl.ds(pl.multiple_of(r * s_n, s_n), n * s_n)

    def fetch(step, sl):
        return pltpu.make_async_copy(up_ref.at[slab(step * tm, tm)], tile_ref.at[sl], fsem.at[sl])

    def drain_step(sl):
        for _ in range(TOP_K):
            pltpu.make_async_copy(tile_ref.at[sl], xs_ref.at[slab(0, tm)], sem.at[sl]).wait()

    @pl.when(i == 0)
    def _():
        fetch(0, 0).start()

    @pl.when(i < last)
    def _():
        fetch(i + 1, (i + 1) % RING).start()

    fetch(i, slot).wait()

    def issue(r, carry):
        for kk in range(TOP_K):
            pltpu.make_async_copy(tile_ref.at[slot, slab(r)],
                                  xs_ref.at[slab(pos_ref[TOP_K * r + kk])],
                                  sem.at[slot]).start(priority=kk)
        return carry

    lax.fori_loop(0, tm, issue, 0, unroll=8)

    @pl.when(i > 0)
    def _():
        drain_step((i + RING - 1) % RING)

    @pl.when(i == last)
    def _():
        drain_step(slot)
        zero_ref[...] = jnp.zeros(zero_ref.shape, zero_ref.dtype)

        def zero_row(dst):
            return pltpu.make_async_copy(zero_ref.at[slab(0)], xs_ref.at[slab(dst)], zsem)

        def zero_tile(ti):
            return pltpu.make_async_copy(zero_ref, xs_ref.at[slab(ti * te, te)], zsem)

        for ex in range(n_exp):
            lo = start_ref[ex] + cnt_ref[ex]
            hi = start_ref[ex] + ((cnt_ref[ex] + te - 1) // te) * te
            lax.fori_loop(lo, hi, lambda r, c: (zero_row(r).start(), c)[1], 0)
            lax.fori_loop(lo, hi, lambda r, c: (zero_row(r).wait(), c)[1], 0)
        lax.fori_loop(nv_ref[0], nt, lambda ti, c: (zero_tile(ti).start(), c)[1], 0)
        lax.fori_loop(nv_ref[0], nt, lambda ti, c: (zero_tile(ti).wait(), c)[1], 0)


def _dispatch(counts, starts, nvalid, pos_flat, upk, *, tm, te, n_exp, nt, s_n):
    t = upk.shape[0] // s_n
    kern = functools.partial(_dispatch_kernel, tm=tm, te=te, n_exp=n_exp, nt=nt, s_n=s_n)
    grid_spec = pltpu.PrefetchScalarGridSpec(
        num_scalar_prefetch=3,
        grid=(t // tm,),
        in_specs=[
            pl.BlockSpec((TOP_K * tm,), lambda i, *_: (i,), memory_space=pltpu.SMEM),
            pl.BlockSpec(memory_space=pl.ANY),
        ],
        out_specs=pl.BlockSpec(memory_space=pl.ANY),
        scratch_shapes=[pltpu.VMEM((RING, tm * s_n, LANES), upk.dtype),
                        pltpu.VMEM((te * s_n, LANES), upk.dtype),
                        pltpu.SemaphoreType.DMA((RING,)), pltpu.SemaphoreType.DMA((RING,)),
                        pltpu.SemaphoreType.DMA],
    )
    return pl.pallas_call(
        kern,
        grid_spec=grid_spec,
        out_shape=jax.ShapeDtypeStruct((nt * te * s_n, LANES), upk.dtype),
        compiler_params=pltpu.CompilerParams(
            dimension_semantics=("arbitrary",), vmem_limit_bytes=VMEM_LIMIT),
        name="moe_dispatch",
    )(counts, starts, nvalid, pos_flat, upk)


def _expert_kernel(te_ref, nv_ref, xs_ref, wg_ref, wu_ref, wd_ref, y_ref, a_ref, acc_ref,
                   *, fc, sc):
    i = pl.program_id(0)
    j = pl.program_id(1)

    @pl.when((i == 0) & (j == 0))
    def _():
        acc_ref[...] = jnp.zeros(acc_ref.shape, f32)

    @pl.when((i >= nv_ref[0]) & (j == 0))
    def _():
        y_ref[...] = jnp.zeros(y_ref.shape, f32)

    @pl.when(i < nv_ref[0])
    def _():
        x = _from_slabs(xs_ref, *acc_ref.shape).astype(bf16)
        for c in range(fc // sc):
            gate = _dot(x, wg_ref[:, c * sc:(c + 1) * sc])
            up = _dot(x, wu_ref[:, c * sc:(c + 1) * sc])
            a_ref[:, c * sc:(c + 1) * sc] = (_silu(gate) * up).astype(bf16)
        part = _dot(a_ref[...], wd_ref[...])
        acc = jnp.where(j == 0, part, acc_ref[...] + part)
        acc_ref[...] = acc
        _to_slabs(y_ref, acc)


def _experts(tile_e, nvalid, xs, w_gu, w_down, *, te, fc, sc):
    n_exp, d, f2 = w_gu.shape
    s_n = d // LANES
    f = f2 // 2
    nf = f // fc
    nt = tile_e.shape[0]

    def jj(i, j, nv):
        return jnp.where(i < nv[0], j, nf - 1)

    grid_spec = pltpu.PrefetchScalarGridSpec(
        num_scalar_prefetch=2,
        grid=(nt, nf),
        in_specs=[
            pl.BlockSpec((te * s_n, LANES), lambda i, j, e, nv: (i, 0)),
            pl.BlockSpec((None, d, fc), lambda i, j, e, nv: (e[i], 0, jj(i, j, nv))),
            pl.BlockSpec((None, d, fc), lambda i, j, e, nv: (e[i], 0, nf + jj(i, j, nv))),
            pl.BlockSpec((None, fc, d), lambda i, j, e, nv: (e[i], jj(i, j, nv), 0)),
        ],
        out_specs=pl.BlockSpec((te * s_n, LANES), lambda i, j, e, nv: (i, 0)),
        scratch_shapes=[pltpu.VMEM((te, fc), bf16), pltpu.VMEM((te, d), f32)],
    )
    kern = functools.partial(_expert_kernel, fc=fc, sc=sc)
    return pl.pallas_call(
        kern,
        grid_spec=grid_spec,
        out_shape=jax.ShapeDtypeStruct(xs.shape, f32),
        compiler_params=pltpu.CompilerParams(
            dimension_semantics=("arbitrary", "arbitrary"), vmem_limit_bytes=VMEM_LIMIT),
        name="moe_experts",
    )(tile_e, nvalid, xs, w_gu, w_gu, w_down)


def _combine_kernel(pos_ref, posn_ref, y_ref, gate_ref, h_ref, gpost_ref, o_ref, buf_ref, sem,
                    *, tm, s_n):
    i = pl.program_id(0)
    slot = i % 2

    def slab(r, n=1):
        return pl.ds(pl.multiple_of(r * s_n, s_n), n * s_n)

    def issue(p_ref, sl):
        def body(r, carry):
            for kk in range(TOP_K):
                pltpu.make_async_copy(y_ref.at[slab(p_ref[TOP_K * r + kk])],
                                      buf_ref.at[sl, kk, slab(r)], sem.at[sl]).start(priority=kk)
            return carry
        lax.fori_loop(0, tm, body, 0, unroll=8)

    @pl.when(i == 0)
    def _():
        issue(pos_ref, 0)

    @pl.when(i + 1 < pl.num_programs(0))
    def _():
        issue(posn_ref, 1 - slot)

    for kk in range(TOP_K):
        pltpu.make_async_copy(y_ref.at[slab(0, tm)], buf_ref.at[slot, kk], sem.at[slot]).wait()
    gates = gate_ref[...]
    mix = (gates[:, 0:1] * _from_slabs(buf_ref.at[slot, 0], *h_ref.shape)
           + gates[:, 1:2] * _from_slabs(buf_ref.at[slot, 1], *h_ref.shape))
    o_ref[...] = h_ref[...] + _rms(mix, gpost_ref[...])


def _combine(pos_flat, y, gates, h, gpost, *, tm):
    t, d = h.shape
    s_n = d // LANES
    kern = functools.partial(_combine_kernel, tm=tm, s_n=s_n)
    n_steps = t // tm
    return pl.pallas_call(
        kern,
        grid=(n_steps,),
        in_specs=[
            pl.BlockSpec((TOP_K * tm,), lambda i: (i,), memory_space=pltpu.SMEM),
            pl.BlockSpec((TOP_K * tm,), lambda i: (jnp.minimum(i + 1, n_steps - 1),),
                         memory_space=pltpu.SMEM),
            pl.BlockSpec(memory_space=pl.ANY),
            pl.BlockSpec((tm, ROUTE_LANES), lambda i: (i, 0)),
            pl.BlockSpec((tm, d), lambda i: (i, 0)),
            _const_spec((1, d)),
        ],
        out_specs=pl.BlockSpec((tm, d), lambda i: (i, 0)),
        out_shape=jax.ShapeDtypeStruct((t, d), f32),
        scratch_shapes=[pltpu.VMEM((2, TOP_K, tm * s_n, LANES), f32),
                        pltpu.SemaphoreType.DMA((2,))],
        compiler_params=pltpu.CompilerParams(
            dimension_semantics=("arbitrary",), vmem_limit_bytes=VMEM_LIMIT),
        name="moe_combine",
    )(pos_flat, pos_flat, y, gates, h, gpost.reshape(1, d))


def _moe(h, gpre, gpost, w_router, w_gu, w_down, *, tm_route, tm_move, te, fc, sc):
    t, d = h.shape
    n_exp = w_router.shape[1]
    upk, pos, gates, counts = _router(h, gpre, w_router, tm=tm_route)
    counts = counts[0, :n_exp]

    nt = (TOP_K * t) // te + n_exp
    tiles_per = (counts + te - 1) // te
    tile_start = jnp.cumsum(tiles_per) - tiles_per
    nvalid = jnp.sum(tiles_per).reshape(1).astype(jnp.int32)
    idx = jnp.minimum(jnp.arange(nt, dtype=jnp.int32), jnp.maximum(nvalid - 1, 0))
    tile_e = (jnp.sum(idx[:, None] >= tile_start[None, :], axis=1) - 1).astype(jnp.int32)
    row_start = (tile_start * te).astype(jnp.int32)
    pos_flat = (pos[:, :TOP_K] + row_start[pos[:, TOP_K:2 * TOP_K]]).reshape(-1)

    xs = _dispatch(counts, row_start, nvalid, pos_flat, upk, tm=tm_move, te=te, n_exp=n_exp, nt=nt,
                   s_n=d // LANES)
    y = _experts(tile_e, nvalid, xs, w_gu, w_down, te=te, fc=fc, sc=sc)
    return _combine(pos_flat, y, gates, h, gpost, tm=tm_move)


def _pick(n, pref):
    return pref if n % pref == 0 else n


def kernel(x, norm_g, conv_w_in, conv_b_in, conv_dw_w, conv_dw_b, conv_ln_g, conv_ln_b, conv_w_out, conv_b_out, hgrn_w_in, hgrn_lower_bounds, hgrn_norm_g, hgrn_w_out, ffn_w_gu, ffn_w_down, moe_router, moe_w_gu, moe_w_down):
    b, l, d = x.shape
    depth = norm_g.shape[0]
    t = b * l
    tm_seq = _pick(l, 512)
    tm_tok = _pick(t, 512)
    h = x
    for i in range(depth):
        j = i // 2
        if i % 2 == 0:
            h = _conv_mixer(h, norm_g[i, 0], norm_g[i, 1], conv_w_in[j].astype(bf16), conv_b_in[j],
                            conv_dw_w[j], conv_dw_b[j], conv_ln_g[j], conv_ln_b[j],
                            conv_w_out[j].astype(bf16), conv_b_out[j], tm=_pick(l, 1024))
            f = ffn_w_down.shape[1]
            h = _ffn(h.reshape(t, d), norm_g[i, 2], norm_g[i, 3], ffn_w_gu[j].astype(bf16),
                     ffn_w_down[j].astype(bf16), tm=_pick(t, 1024), fc=_pick(f, 256)).reshape(b, l, d)
        else:
            h, w_gu_bf, w_down_bf = _hgrn_mixer(
                h, norm_g[i, 0], norm_g[i, 1], hgrn_w_in[j].astype(bf16), hgrn_lower_bounds,
                hgrn_norm_g[j], hgrn_w_out[j].astype(bf16), moe_w_gu[j], moe_w_down[j],
                tm=tm_seq, layer=i)
            f = moe_w_down.shape[2]
            fc = f // 2 if (f // 2) % LANES == 0 else f
            h = _moe(h.reshape(t, d), norm_g[i, 2], norm_g[i, 3], moe_router[j],
                     w_gu_bf, w_down_bf,
                     tm_route=tm_tok, tm_move=tm_tok, te=_pick(t, 512), fc=fc,
                     sc=_pick(fc, 256)).reshape(b, l, d)
    return h
```

```python
import functools

import jax
import jax.numpy as jnp
from jax import lax
from jax.experimental import pallas as pl
from jax.experimental.pallas import tpu as pltpu

EPS = 1e-6
CHUNK = 64
SUB = 16
HEAD = 128
CONV_WIDTH = 31
HALO = 32
CONV_ROWS = 64
LN_ROWS = 32
ROW_PITCH = 2
LANES = 128
SUBLANES = 8
TOP_K = 2
ROUTE_LANES = 8
RING = 3
EXP_CLAMP = 80.0
VMEM_LIMIT = 56 * 1024 * 1024

f32 = jnp.float32
bf16 = jnp.bfloat16


def _dot(a, b):
    return jnp.dot(a, b, preferred_element_type=f32)


def _dot_nt(a, b):
    return lax.dot_general(a, b, (((1,), (1,)), ((), ())), preferred_element_type=f32)


def _dot_tn(a, b):
    return lax.dot_general(a, b, (((0,), (0,)), ((), ())), preferred_element_type=f32)


def _rms(x, g):
    return x * lax.rsqrt(jnp.mean(x * x, axis=-1, keepdims=True) + EPS) * g


def _silu(x):
    return x * jax.nn.sigmoid(x)


def _to_slabs(ref, x):
    n, d = x.shape
    s_n = d // LANES
    for s in range(s_n):
        ref[pl.ds(s, n, stride=s_n), :] = x[:, s * LANES:(s + 1) * LANES]


def _from_slabs(ref, n, d):
    s_n = d // LANES
    return jnp.concatenate([ref[pl.ds(s, n, stride=s_n), :] for s in range(s_n)], axis=1)


def _split_bf16(x):
    hi = x.astype(bf16)
    return hi, (x - hi.astype(f32)).astype(bf16)


def _const_spec(shape):
    nd = len(shape)
    return pl.BlockSpec(shape, lambda *_: (0,) * nd, pipeline_mode=pl.Buffered(1))


def _conv_mixer_kernel(x_ref, gpre_ref, gpost_ref, win_ref, bin_ref, dww_ref, dwb_ref,
                       lng_ref, lnb_ref, wout_ref, bout_ref, o_ref, gl_ref, y_ref, z_ref,
                       *, tm, d):
    i = pl.program_id(1)
    nslab = d // LANES

    def rows(start, n):
        return pl.ds(ROW_PITCH * start, n, stride=ROW_PITCH)

    @pl.when(i == 0)
    def _():
        for c in range(nslab):
            gl_ref[c, rows(0, HALO), :] = jnp.zeros((HALO, LANES), f32)

    @pl.when(i > 0)
    def _():
        for c in range(nslab):
            gl_ref[c, rows(0, HALO), :] = gl_ref[c, rows(tm, HALO), :]

    x = x_ref[...]
    u = _rms(x, gpre_ref[...]).astype(bf16)
    a = _dot(u, win_ref[:, :d]) + bin_ref[:, :d]
    gate = _dot(u, win_ref[:, d:]) + bin_ref[:, d:]
    glu = a * jax.nn.sigmoid(gate)
    for c in range(nslab):
        gl_ref[c, rows(HALO, tm), :] = glu[:, c * LANES:(c + 1) * LANES]

    off = HALO - (CONV_WIDTH - 1)
    for c in range(nslab):
        cols = slice(c * LANES, (c + 1) * LANES)
        for rb in range(tm // CONV_ROWS):
            base = rb * CONV_ROWS
            acc = jnp.broadcast_to(dwb_ref[:, cols], (CONV_ROWS, LANES))
            for j in range(CONV_WIDTH):
                acc = acc + dww_ref[j:j + 1, cols] * gl_ref[c, rows(base + off + j, CONV_ROWS), :]
            y_ref[base:base + CONV_ROWS, cols] = acc

    for rb in range(tm // LN_ROWS):
        base = rb * LN_ROWS
        acc = y_ref[base:base + LN_ROWS, :]
        mu = jnp.mean(acc, axis=-1, keepdims=True)
        xc = acc - mu
        y = xc * lax.rsqrt(jnp.mean(xc * xc, axis=-1, keepdims=True) + EPS)
        y = y * lng_ref[...] + lnb_ref[...]
        z_ref[base:base + LN_ROWS, :] = _silu(y).astype(bf16)

    out = _dot(z_ref[...], wout_ref[...]) + bout_ref[...]
    o_ref[...] = x + _rms(out, gpost_ref[...])


def _conv_mixer(x, gpre, gpost, w_in, b_in, dw_w, dw_b, ln_g, ln_b, w_out, b_out, *, tm):
    b, l, d = x.shape
    row = lambda v: v.reshape(1, -1)
    kern = functools.partial(_conv_mixer_kernel, tm=tm, d=d)
    return pl.pallas_call(
        kern,
        grid=(b, l // tm),
        in_specs=[
            pl.BlockSpec((None, tm, d), lambda bi, i: (bi, i, 0)),
            _const_spec((1, d)), _const_spec((1, d)),
            _const_spec((d, 2 * d)), _const_spec((1, 2 * d)),
            _const_spec((CONV_WIDTH, d)), _const_spec((1, d)),
            _const_spec((1, d)), _const_spec((1, d)),
            _const_spec((d, d)), _const_spec((1, d)),
        ],
        out_specs=pl.BlockSpec((None, tm, d), lambda bi, i: (bi, i, 0)),
        out_shape=jax.ShapeDtypeStruct((b, l, d), f32),
        scratch_shapes=[pltpu.VMEM((d // LANES, ROW_PITCH * (tm + HALO), LANES), f32),
                        pltpu.VMEM((tm, d), f32), pltpu.VMEM((tm, d), bf16)],
        compiler_params=pltpu.CompilerParams(
            dimension_semantics=("arbitrary", "arbitrary"), vmem_limit_bytes=VMEM_LIMIT),
        name="conv_mixer",
    )(x, row(gpre), row(gpost), w_in, row(b_in), dw_w, row(dw_b), row(ln_g), row(ln_b),
      w_out, row(b_out))


def _ffn_kernel(h_ref, gpre_ref, gpost_ref, wgu_ref, wd_ref, o_ref, a_ref, *, f, fc):
    x = h_ref[...]
    u = _rms(x, gpre_ref[...]).astype(bf16)
    for c in range(f // fc):
        gate = _dot(u, wgu_ref[:, c * fc:(c + 1) * fc])
        up = _dot(u, wgu_ref[:, f + c * fc:f + (c + 1) * fc])
        a_ref[:, c * fc:(c + 1) * fc] = (_silu(gate) * up).astype(bf16)
    out = _dot(a_ref[...], wd_ref[...])
    o_ref[...] = x + _rms(out, gpost_ref[...])


def _ffn(h, gpre, gpost, w_gu, w_down, *, tm, fc):
    t, d = h.shape
    f = w_down.shape[0]
    kern = functools.partial(_ffn_kernel, f=f, fc=fc)
    return pl.pallas_call(
        kern,
        grid=(t // tm,),
        in_specs=[
            pl.BlockSpec((tm, d), lambda i: (i, 0)),
            _const_spec((1, d)), _const_spec((1, d)),
            _const_spec((d, 2 * f)), _const_spec((f, d)),
        ],
        out_specs=pl.BlockSpec((tm, d), lambda i: (i, 0)),
        out_shape=jax.ShapeDtypeStruct((t, d), f32),
        scratch_shapes=[pltpu.VMEM((tm, f), bf16)],
        compiler_params=pltpu.CompilerParams(
            dimension_semantics=("arbitrary",), vmem_limit_bytes=VMEM_LIMIT),
        name="dense_ffn",
    )(h, gpre.reshape(1, d), gpost.reshape(1, d), w_gu, w_down)


def _hgrn_kernel(h_ref, gpre_ref, gpost_ref, win_ref, lbp_ref, ng_ref, wout_ref, wgu_ref, wdn_ref,
                 o_ref, wgu_bf_ref, wdn_bf_ref,
                 q_ref, k_ref, v_ref, g_ref, og_ref, oh_ref, st_ref, *, tm, d, layer):
    i = pl.program_id(1)
    heads = d // HEAD
    nsub = CHUNK // SUB

    wgu_bf_ref[...] = wgu_ref[...].astype(bf16)
    wdn_bf_ref[...] = wdn_ref[...].astype(bf16)

    @pl.when(i == 0)
    def _():
        st_ref[...] = jnp.zeros(st_ref.shape, f32)

    x = h_ref[...]
    u = _rms(x, gpre_ref[...]).astype(bf16)

    lbp = lbp_ref[...]
    e = jnp.exp(lbp - jnp.max(lbp, axis=0, keepdims=True))
    sm = e / jnp.sum(e, axis=0, keepdims=True)
    lb = jnp.sum(sm[1:layer + 1, :], axis=0, keepdims=True)

    q_ref[...] = _silu(_dot(u, win_ref[:, 0:d]))
    forget = lb + (1.0 - lb) * jax.nn.sigmoid(_dot(u, win_ref[:, d:2 * d]))
    k_ref[...] = 1.0 - forget
    g_ref[...] = jnp.log(forget)
    v_ref[...] = _dot(u, win_ref[:, 2 * d:3 * d]).astype(bf16)
    og_ref[...] = _silu(_dot(u, win_ref[:, 3 * d:4 * d]))

    rr = lax.broadcasted_iota(jnp.int32, (CHUNK, CHUNK), 0)
    cc = lax.broadcasted_iota(jnp.int32, (CHUNK, CHUNK), 1)
    tri = (rr >= cc).astype(bf16)
    diag_mask = (rr >= cc) & ((rr // SUB) == (cc // SUB))
    ng = ng_ref[...]
    zeros_blk = jnp.zeros((SUB, d), f32)

    def block_rows(vals):
        return jnp.concatenate([jnp.broadcast_to(r, (SUB, d)) for r in vals], axis=0)

    def chunk_body(c, carry):
        r0 = pl.multiple_of(c * CHUNK, CHUNK)
        rows = pl.ds(r0, CHUNK)
        g_hi, g_lo = _split_bf16(g_ref[rows, :])
        big_g = _dot(tri, g_hi) + _dot(tri, g_lo)
        q = q_ref[rows, :]
        k = k_ref[rows, :]

        ends = [big_g[SUB * b + SUB - 1:SUB * b + SUB, :] for b in range(nsub)]
        prevs = [jnp.zeros((1, d), f32)] + ends[:-1]
        ref_prev = block_rows(prevs)
        ref_end = block_rows(ends)
        qw = q * jnp.exp(big_g - ref_prev)
        k_end = k * jnp.exp(ref_end - big_g)
        qe = (qw * block_rows([jnp.exp(p) for p in prevs])).astype(bf16)
        k_dec = (k_end * block_rows([jnp.exp(ends[-1] - e) for e in ends])).astype(bf16)
        half = 0.5 * (ref_prev - ref_end)
        q_mid = (q * jnp.exp(jnp.clip(big_g - ref_prev + half, -EXP_CLAMP, EXP_CLAMP))).astype(bf16)
        k_mid = (k * jnp.exp(jnp.clip(ref_end - big_g + half, -EXP_CLAMP, EXP_CLAMP))).astype(bf16)
        q_parts, k_parts = [], []
        for jb in range(nsub - 1):
            qp = [zeros_blk] * (jb + 1)
            for b in range(jb + 1, nsub):
                piece = qw[SUB * b:SUB * (b + 1), :]
                if b > jb + 1:
                    piece = piece * jnp.exp(prevs[b] - ends[jb])
                qp.append(piece)
            q_parts.append(jnp.concatenate(qp, axis=0).astype(bf16))
            kp = [zeros_blk] * nsub
            kp[jb] = k_end[SUB * jb:SUB * (jb + 1), :]
            k_parts.append(jnp.concatenate(kp, axis=0).astype(bf16))
        decay = jnp.exp(ends[-1])

        for hd in range(heads):
            sl = slice(hd * HEAD, (hd + 1) * HEAD)
            v = v_ref[rows, sl]
            st = st_ref[hd]
            s_off = _dot_nt(jnp.concatenate([p[:, sl] for p in q_parts], axis=1),
                            jnp.concatenate([p[:, sl] for p in k_parts], axis=1))
            s_diag = _dot_nt(q_mid[:, sl], k_mid[:, sl])
            scores = (s_off + jnp.where(diag_mask, s_diag, 0.0)).astype(bf16)
            o = _dot_nt(qe[:, sl], st.astype(bf16)) + _dot(scores, v)
            st_ref[hd] = decay[:, sl] * st + _dot_tn(v, k_dec[:, sl])
            o = o * lax.rsqrt(jnp.mean(o * o, axis=-1, keepdims=True) + EPS) * ng
            oh_ref[rows, sl] = (o * og_ref[rows, sl]).astype(bf16)
        return carry

    lax.fori_loop(0, tm // CHUNK, chunk_body, 0, unroll=True)
    out = _dot(oh_ref[...], wout_ref[...])
    o_ref[...] = x + _rms(out, gpost_ref[...])


def _hgrn_mixer(h, gpre, gpost, w_in, lower_bounds, norm_g, w_out, moe_w_gu, moe_w_down,
                *, tm, layer):
    b, l, d = h.shape
    depth = lower_bounds.shape[0]
    n_tiles = l // tm
    steps = b * n_tiles
    wgu = moe_w_gu.reshape(-1, moe_w_gu.shape[-1])
    wdn = moe_w_down.reshape(-1, moe_w_down.shape[-1])
    gu_rows, dn_rows = wgu.shape[0] // steps, wdn.shape[0] // steps
    assert gu_rows * steps == wgu.shape[0] and dn_rows * steps == wdn.shape[0]
    slice_map = lambda bi, i: (bi * n_tiles + i, 0)
    kern = functools.partial(_hgrn_kernel, tm=tm, d=d, layer=layer)
    out, wgu_bf, wdn_bf = pl.pallas_call(
        kern,
        grid=(b, n_tiles),
        in_specs=[
            pl.BlockSpec((None, tm, d), lambda bi, i: (bi, i, 0)),
            _const_spec((1, d)), _const_spec((1, d)),
            _const_spec((d, 4 * d)), _const_spec((depth, d)), _const_spec((1, HEAD)),
            _const_spec((d, d)),
            pl.BlockSpec((gu_rows, wgu.shape[1]), slice_map),
            pl.BlockSpec((dn_rows, wdn.shape[1]), slice_map),
        ],
        out_specs=[
            pl.BlockSpec((None, tm, d), lambda bi, i: (bi, i, 0)),
            pl.BlockSpec((gu_rows, wgu.shape[1]), slice_map),
            pl.BlockSpec((dn_rows, wdn.shape[1]), slice_map),
        ],
        out_shape=[
            jax.ShapeDtypeStruct((b, l, d), f32),
            jax.ShapeDtypeStruct(wgu.shape, bf16),
            jax.ShapeDtypeStruct(wdn.shape, bf16),
        ],
        scratch_shapes=[pltpu.VMEM((tm, d), f32), pltpu.VMEM((tm, d), f32),
                        pltpu.VMEM((tm, d), bf16), pltpu.VMEM((tm, d), f32),
                        pltpu.VMEM((tm, d), f32), pltpu.VMEM((tm, d), bf16),
                        pltpu.VMEM((d // HEAD, HEAD, HEAD), f32)],
        compiler_params=pltpu.CompilerParams(
            dimension_semantics=("arbitrary", "arbitrary"), vmem_limit_bytes=VMEM_LIMIT),
        name="hgrn_mixer",
    )(h, gpre.reshape(1, d), gpost.reshape(1, d), w_in, lower_bounds, norm_g.reshape(1, HEAD), w_out,
      wgu, wdn)
    return out, wgu_bf.reshape(moe_w_gu.shape), wdn_bf.reshape(moe_w_down.shape)


def _router_kernel(h_ref, gpre_ref, wr_ref, up_ref, pos_ref, gate_ref, cnt_ref, carry_ref,
                   *, tm, d, n_exp):
    i = pl.program_id(0)

    @pl.when(i == 0)
    def _():
        carry_ref[...] = jnp.zeros(carry_ref.shape, f32)

    u = _rms(h_ref[...], gpre_ref[...])
    _to_slabs(up_ref, u)

    u_hi, u_lo = _split_bf16(u)
    w_hi, w_lo = _split_bf16(wr_ref[...])
    logits = _dot(u_hi, w_hi) + (_dot(u_hi, w_lo) + _dot(u_lo, w_hi))
    lane = lax.broadcasted_iota(jnp.int32, (tm, LANES), 1)
    neg = jnp.float32(-jnp.inf)
    lg = jnp.where(lane < n_exp, logits, neg)
    v1 = jnp.max(lg, axis=-1, keepdims=True)
    i1 = jnp.min(jnp.where(lg == v1, lane, LANES), axis=-1, keepdims=True)
    m1 = lane == i1
    lg2 = jnp.where(m1, neg, lg)
    v2 = jnp.max(lg2, axis=-1, keepdims=True)
    i2 = jnp.min(jnp.where(lg2 == v2, lane, LANES), axis=-1, keepdims=True)
    m2 = lane == i2
    dd = jnp.exp(v2 - v1)
    w1 = 1.0 / (1.0 + dd)
    w2 = dd / (1.0 + dd)

    sel = (m1 | m2).astype(f32)
    rr = lax.broadcasted_iota(jnp.int32, (tm, tm), 0)
    cc = lax.broadcasted_iota(jnp.int32, (tm, tm), 1)
    cum = _dot((rr >= cc).astype(bf16), sel.astype(bf16))
    rank = cum - sel + carry_ref[...]
    r1 = jnp.sum(jnp.where(m1, rank, 0.0), axis=-1, keepdims=True).astype(jnp.int32)
    r2 = jnp.sum(jnp.where(m2, rank, 0.0), axis=-1, keepdims=True).astype(jnp.int32)
    pos_ref[...] = jnp.where(lane == 0, r1, jnp.where(lane == 1, r2, jnp.where(
        lane == 2, i1, jnp.where(lane == 3, i2, 0))))[:, :ROUTE_LANES]
    gate_ref[...] = jnp.where(lane == 0, w1, jnp.where(lane == 1, w2, 0.0))[:, :ROUTE_LANES]
    carry_ref[...] = carry_ref[...] + cum[tm - 1:tm, :]
    cnt_ref[...] = carry_ref[...].astype(jnp.int32)


def _router(h, gpre, w_router, *, tm):
    t, d = h.shape
    n_exp = w_router.shape[1]
    wr = jnp.zeros((d, LANES), f32).at[:, :n_exp].set(w_router)
    kern = functools.partial(_router_kernel, tm=tm, d=d, n_exp=n_exp)
    return pl.pallas_call(
        kern,
        grid=(t // tm,),
        in_specs=[
            pl.BlockSpec((tm, d), lambda i: (i, 0)),
            _const_spec((1, d)), _const_spec((d, LANES)),
        ],
        out_specs=[
            pl.BlockSpec((tm * (d // LANES), LANES), lambda i: (i, 0)),
            pl.BlockSpec((tm, ROUTE_LANES), lambda i: (i, 0)),
            pl.BlockSpec((tm, ROUTE_LANES), lambda i: (i, 0)),
            pl.BlockSpec((1, LANES), lambda i: (0, 0)),
        ],
        out_shape=[
            jax.ShapeDtypeStruct((t * (d // LANES), LANES), f32),
            jax.ShapeDtypeStruct((t, ROUTE_LANES), jnp.int32),
            jax.ShapeDtypeStruct((t, ROUTE_LANES), f32),
            jax.ShapeDtypeStruct((1, LANES), jnp.int32),
        ],
        scratch_shapes=[pltpu.VMEM((1, LANES), f32)],
        compiler_params=pltpu.CompilerParams(
            dimension_semantics=("arbitrary",), vmem_limit_bytes=VMEM_LIMIT),
        name="moe_router",
    )(h, gpre.reshape(1, d), wr)


def _dispatch_kernel(cnt_ref, start_ref, nv_ref, pos_ref, up_ref, xs_ref, tile_ref, zero_ref,
                     fsem, sem, zsem, *, tm, te, n_exp, nt, s_n):
    i = pl.program_id(0)
    last = pl.num_programs(0) - 1
    slot = i % RING

    def slab(r, n=1):
        return pl.ds(pl.multiple_of(r * s_n, s_n), n * s_n)

    def fetch(step, sl):
        return pltpu.make_async_copy(up_ref.at[slab(step * tm, tm)], tile_ref.at[sl], fsem.at[sl])

    def drain_step(sl):
        for _ in range(TOP_K):
            pltpu.make_async_copy(tile_ref.at[sl], xs_ref.at[slab(0, tm)], sem.at[sl]).wait()

    @pl.when(i == 0)
    def _():
        fetch(0, 0).start()

    @pl.when(i < last)
    def _():
        fetch(i + 1, (i + 1) % RING).start()

    fetch(i, slot).wait()

    def issue(r, carry):
        for kk in range(TOP_K):
            pltpu.make_async_copy(tile_ref.at[slot, slab(r)],
                                  xs_ref.at[slab(pos_ref[TOP_K * r + kk])],
                                  sem.at[slot]).start(priority=kk)
        return carry

    lax.fori_loop(0, tm, issue, 0, unroll=8)

    @pl.when(i > 0)
    def _():
        drain_step((i + RING - 1) % RING)

    @pl.when(i == last)
    def _():
        drain_step(slot)
        zero_ref[...] = jnp.zeros(zero_ref.shape, zero_ref.dtype)

        def zero_row(dst):
            return pltpu.make_async_copy(zero_ref.at[slab(0)], xs_ref.at[slab(dst)], zsem)

        def zero_tile(ti):
            return pltpu.make_async_copy(zero_ref, xs_ref.at[slab(ti * te, te)], zsem)

        for ex in range(n_exp):
            lo = start_ref[ex] + cnt_ref[ex]
            hi = start_ref[ex] + ((cnt_ref[ex] + te - 1) // te) * te
            lax.fori_loop(lo, hi, lambda r, c: (zero_row(r).start(), c)[1], 0)
            lax.fori_loop(lo, hi, lambda r, c: (zero_row(r).wait(), c)[1], 0)
        lax.fori_loop(nv_ref[0], nt, lambda ti, c: (zero_tile(ti).start(), c)[1], 0)
        lax.fori_loop(nv_ref[0], nt, lambda ti, c: (zero_tile(ti).wait(), c)[1], 0)


def _dispatch(counts, starts, nvalid, pos_flat, upk, *, tm, te, n_exp, nt, s_n):
    t = upk.shape[0] // s_n
    kern = functools.partial(_dispatch_kernel, tm=tm, te=te, n_exp=n_exp, nt=nt, s_n=s_n)
    grid_spec = pltpu.PrefetchScalarGridSpec(
        num_scalar_prefetch=3,
        grid=(t // tm,),
        in_specs=[
            pl.BlockSpec((TOP_K * tm,), lambda i, *_: (i,), memory_space=pltpu.SMEM),
            pl.BlockSpec(memory_space=pl.ANY),
        ],
        out_specs=pl.BlockSpec(memory_space=pl.ANY),
        scratch_shapes=[pltpu.VMEM((RING, tm * s_n, LANES), upk.dtype),
                        pltpu.VMEM((te * s_n, LANES), upk.dtype),
                        pltpu.SemaphoreType.DMA((RING,)), pltpu.SemaphoreType.DMA((RING,)),
                        pltpu.SemaphoreType.DMA],
    )
    return pl.pallas_call(
        kern,
        grid_spec=grid_spec,
        out_shape=jax.ShapeDtypeStruct((nt * te * s_n, LANES), upk.dtype),
        compiler_params=pltpu.CompilerParams(
            dimension_semantics=("arbitrary",), vmem_limit_bytes=VMEM_LIMIT),
        name="moe_dispatch",
    )(counts, starts, nvalid, pos_flat, upk)


def _expert_kernel(te_ref, nv_ref, xs_ref, wg_ref, wu_ref, wd_ref, y_ref, a_ref, acc_ref,
                   *, fc, sc):
    i = pl.program_id(0)
    j = pl.program_id(1)

    @pl.when((i == 0) & (j == 0))
    def _():
        acc_ref[...] = jnp.zeros(acc_ref.shape, f32)

    @pl.when((i >= nv_ref[0]) & (j == 0))
    def _():
        y_ref[...] = jnp.zeros(y_ref.shape, f32)

    @pl.when(i < nv_ref[0])
    def _():
        x = _from_slabs(xs_ref, *acc_ref.shape).astype(bf16)
        for c in range(fc // sc):
            gate = _dot(x, wg_ref[:, c * sc:(c + 1) * sc])
            up = _dot(x, wu_ref[:, c * sc:(c + 1) * sc])
            a_ref[:, c * sc:(c + 1) * sc] = (_silu(gate) * up).astype(bf16)
        part = _dot(a_ref[...], wd_ref[...])
        acc = jnp.where(j == 0, part, acc_ref[...] + part)
        acc_ref[...] = acc
        _to_slabs(y_ref, acc)


def _experts(tile_e, nvalid, xs, w_gu, w_down, *, te, fc, sc):
    n_exp, d, f2 = w_gu.shape
    s_n = d // LANES
    f = f2 // 2
    nf = f // fc
    nt = tile_e.shape[0]

    def jj(i, j, nv):
        return jnp.where(i < nv[0], j, nf - 1)

    grid_spec = pltpu.PrefetchScalarGridSpec(
        num_scalar_prefetch=2,
        grid=(nt, nf),
        in_specs=[
            pl.BlockSpec((te * s_n, LANES), lambda i, j, e, nv: (i, 0)),
            pl.BlockSpec((None, d, fc), lambda i, j, e, nv: (e[i], 0, jj(i, j, nv))),
            pl.BlockSpec((None, d, fc), lambda i, j, e, nv: (e[i], 0, nf + jj(i, j, nv))),
            pl.BlockSpec((None, fc, d), lambda i, j, e, nv: (e[i], jj(i, j, nv), 0)),
        ],
        out_specs=pl.BlockSpec((te * s_n, LANES), lambda i, j, e, nv: (i, 0)),
        scratch_shapes=[pltpu.VMEM((te, fc), bf16), pltpu.VMEM((te, d), f32)],
    )
    kern = functools.partial(_expert_kernel, fc=fc, sc=sc)
    return pl.pallas_call(
        kern,
        grid_spec=grid_spec,
        out_shape=jax.ShapeDtypeStruct(xs.shape, f32),
        compiler_params=pltpu.CompilerParams(
            dimension_semantics=("arbitrary", "arbitrary"), vmem_limit_bytes=VMEM_LIMIT),
        name="moe_experts",
    )(tile_e, nvalid, xs, w_gu, w_gu, w_down)


def _combine_kernel(pos_ref, posn_ref, y_ref, gate_ref, h_ref, gpost_ref, o_ref, buf_ref, sem,
                    *, tm, s_n):
    i = pl.program_id(0)
    slot = i % 2

    def slab(r, n=1):
        return pl.ds(pl.multiple_of(r * s_n, s_n), n * s_n)

    def issue(p_ref, sl):
        def body(r, carry):
            for kk in range(TOP_K):
                pltpu.make_async_copy(y_ref.at[slab(p_ref[TOP_K * r + kk])],
                                      buf_ref.at[sl, kk, slab(r)], sem.at[sl]).start(priority=kk)
            return carry
        lax.fori_loop(0, tm, body, 0, unroll=8)

    @pl.when(i == 0)
    def _():
        issue(pos_ref, 0)

    @pl.when(i + 1 < pl.num_programs(0))
    def _():
        issue(posn_ref, 1 - slot)

    for kk in range(TOP_K):
        pltpu.make_async_copy(y_ref.at[slab(0, tm)], buf_ref.at[slot, kk], sem.at[slot]).wait()
    gates = gate_ref[...]
    mix = (gates[:, 0:1] * _from_slabs(buf_ref.at[slot, 0], *h_ref.shape)
           + gates[:, 1:2] * _from_slabs(buf_ref.at[slot, 1], *h_ref.shape))
    o_ref[...] = h_ref[...] + _rms(mix, gpost_ref[...])


def _combine(pos_flat, y, gates, h, gpost, *, tm):
    t, d = h.shape
    s_n = d // LANES
    kern = functools.partial(_combine_kernel, tm=tm, s_n=s_n)
    n_steps = t // tm
    return pl.pallas_call(
        kern,
        grid=(n_steps,),
        in_specs=[
            pl.BlockSpec((TOP_K * tm,), lambda i: (i,), memory_space=pltpu.SMEM),
            pl.BlockSpec((TOP_K * tm,), lambda i: (jnp.minimum(i + 1, n_steps - 1),),
                         memory_space=pltpu.SMEM),
            pl.BlockSpec(memory_space=pl.ANY),
            pl.BlockSpec((tm, ROUTE_LANES), lambda i: (i, 0)),
            pl.BlockSpec((tm, d), lambda i: (i, 0)),
            _const_spec((1, d)),
        ],
        out_specs=pl.BlockSpec((tm, d), lambda i: (i, 0)),
        out_shape=jax.ShapeDtypeStruct((t, d), f32),
        scratch_shapes=[pltpu.VMEM((2, TOP_K, tm * s_n, LANES), f32),
                        pltpu.SemaphoreType.DMA((2,))],
        compiler_params=pltpu.CompilerParams(
            dimension_semantics=("arbitrary",), vmem_limit_bytes=VMEM_LIMIT),
        name="moe_combine",
    )(pos_flat, pos_flat, y, gates, h, gpost.reshape(1, d))


def _moe(h, gpre, gpost, w_router, w_gu, w_down, *, tm_route, tm_move, te, fc, sc):
    t, d = h.shape
    n_exp = w_router.shape[1]
    upk, pos, gates, counts = _router(h, gpre, w_router, tm=tm_route)
    counts = counts[0, :n_exp]

    nt = (TOP_K * t) // te + n_exp
    tiles_per = (counts + te - 1) // te
    tile_start = jnp.cumsum(tiles_per) - tiles_per
    nvalid = jnp.sum(tiles_per).reshape(1).astype(jnp.int32)
    idx = jnp.minimum(jnp.arange(nt, dtype=jnp.int32), jnp.maximum(nvalid - 1, 0))
    tile_e = (jnp.sum(idx[:, None] >= tile_start[None, :], axis=1) - 1).astype(jnp.int32)
    row_start = (tile_start * te).astype(jnp.int32)
    pos_flat = (pos[:, :TOP_K] + row_start[pos[:, TOP_K:2 * TOP_K]]).reshape(-1)

    xs = _dispatch(counts, row_start, nvalid, pos_flat, upk, tm=tm_move, te=te, n_exp=n_exp, nt=nt,
                   s_n=d // LANES)
    y = _experts(tile_e, nvalid, xs, w_gu, w_down, te=te, fc=fc, sc=sc)
    return _combine(pos_flat, y, gates, h, gpost, tm=tm_move)


def _pick(n, pref):
    return pref if n % pref == 0 else n


def kernel(x, norm_g, conv_w_in, conv_b_in, conv_dw_w, conv_dw_b, conv_ln_g, conv_ln_b, conv_w_out, conv_b_out, hgrn_w_in, hgrn_lower_bounds, hgrn_norm_g, hgrn_w_out, ffn_w_gu, ffn_w_down, moe_router, moe_w_gu, moe_w_down):
    b, l, d = x.shape
    depth = norm_g.shape[0]
    t = b * l
    tm_seq = _pick(l, 512)
    tm_tok = _pick(t, 512)
    h = x
    for i in range(depth):
        j = i // 2
        if i % 2 == 0:
            h = _conv_mixer(h, norm_g[i, 0], norm_g[i, 1], conv_w_in[j].astype(bf16), conv_b_in[j],
                            conv_dw_w[j], conv_dw_b[j], conv_ln_g[j], conv_ln_b[j],
                            conv_w_out[j].astype(bf16), conv_b_out[j], tm=_pick(l, 1024))
            f = ffn_w_down.shape[1]
            h = _ffn(h.reshape(t, d), norm_g[i, 2], norm_g[i, 3], ffn_w_gu[j].astype(bf16),
                     ffn_w_down[j].astype(bf16), tm=_pick(t, 1024), fc=_pick(f, 256)).reshape(b, l, d)
        else:
            h, w_gu_bf, w_down_bf = _hgrn_mixer(
                h, norm_g[i, 0], norm_g[i, 1], hgrn_w_in[j].astype(bf16), hgrn_lower_bounds,
                hgrn_norm_g[j], hgrn_w_out[j].astype(bf16), moe_w_gu[j], moe_w_down[j],
                tm=tm_seq, layer=i)
            f = moe_w_down.shape[2]
            fc = f // 2 if (f // 2) % LANES == 0 else f
            h = _moe(h.reshape(t, d), norm_g[i, 2], norm_g[i, 3], moe_router[j],
                     w_gu_bf, w_down_bf,
                     tm_route=tm_tok, tm_move=_pick(t, 1024), te=_pick(t, 512), fc=fc,
                     sc=_pick(fc, 256)).reshape(b, l, d)
    return h
```
